```python
import jax, jax.numpy as jnp
from jax import lax
import numpy as np

D_MODEL = 1024
BATCH = 8
SEQ = 2048
DEPTH = 1
DEC_BATCH = 128
DEC_SEQ = 4
PAST_LEN = 16384
PAGE_SIZE = 128

D_MIX = D_MODEL
D_CONV = D_MIX // 2
D_LRU = D_MIX - D_CONV
CONV_GROUPS = 8
LRU_HEADS = 8
LRU_HEAD_DIM = D_LRU // LRU_HEADS
CONV_WIDTH = 31
LRU_CONV_WIDTH = 4
LRU_C = 8.0
D_FF = 2816
D_PLE = 256
D_IN = 2 * D_CONV + 2 * D_LRU
EPS = 1e-6
FFN_RES_WEIGHT = 0.5

kernel_name = "hymba_conformer_rglru_macaron_step"


def rmsnorm(x, g):
    xf = x.astype(jnp.float32)
    y = xf * lax.rsqrt(jnp.mean(xf * xf, axis=-1, keepdims=True) + EPS)
    return (y * g.astype(jnp.float32)).astype(x.dtype)


def swiglu_ffn(x, w_gu, w_down):
    gu = x @ w_gu
    gate, up = jnp.split(gu, 2, axis=-1)
    return (jax.nn.silu(gate) * up) @ w_down


def causal_dwconv(xp, w, b):
    c = xp.shape[-1]
    y = lax.conv_general_dilated(xp, w[:, None, :].astype(xp.dtype), window_strides=(1,),
                                 padding='VALID', dimension_numbers=('NWC', 'WIO', 'NWC'),
                                 feature_group_count=c)
    return y + b


def lru_combine(left, right):
    a_l, b_l = left
    a_r, b_r = right
    return a_l * a_r, a_r * b_l + b_r


def mixer_block(x, conv_buf, lconv_buf, h0, pre_g, post_g, w_in, conv_w, conv_b, conv_norm_g,
                lru_conv_w, lru_conv_b, lru_wa, lru_ba, lru_wx, lru_bx, lru_lambda, w_out):
    bsz, t, _ = x.shape
    h = rmsnorm(x, pre_g)
    u = h @ w_in
    u_val, u_glu, u_x, u_gelu = jnp.split(u, [D_CONV, 2 * D_CONV, 2 * D_CONV + D_LRU], axis=-1)

    g = u_val * jax.nn.sigmoid(u_glu)
    gp = jnp.concatenate([conv_buf.astype(g.dtype), g], axis=1)
    c = causal_dwconv(gp, conv_w, conv_b)
    c = jax.nn.silu(rmsnorm(c, conv_norm_g))
    new_conv_buf = gp[:, -(CONV_WIDTH - 1):, :]

    xp = jnp.concatenate([lconv_buf.astype(u_x.dtype), u_x], axis=1)
    xc = causal_dwconv(xp, lru_conv_w, lru_conv_b)
    new_lconv_buf = xp[:, -(LRU_CONV_WIDTH - 1):, :]
    xh = xc.reshape(bsz, t, LRU_HEADS, LRU_HEAD_DIM)
    r = jax.nn.sigmoid(jnp.einsum('bthi,hij->bthj', xh, lru_wa).reshape(bsz, t, D_LRU) + lru_ba)
    i_g = jax.nn.sigmoid(jnp.einsum('bthi,hij->bthj', xh, lru_wx).reshape(bsz, t, D_LRU) + lru_bx)
    log_a = -LRU_C * r.astype(jnp.float32) * jax.nn.softplus(-lru_lambda.astype(jnp.float32))
    a = jnp.exp(log_a)
    bterm = jnp.sqrt(-jnp.expm1(2.0 * log_a)) * (i_g * xc).astype(jnp.float32)
    bterm = bterm.at[:, 0, :].add(a[:, 0, :] * h0.astype(jnp.float32))
    _, hs = lax.associative_scan(lru_combine, (a, bterm), axis=1)
    h_last = hs[:, -1, :].astype(h0.dtype)
    yb = hs.astype(x.dtype) * jax.nn.gelu(u_gelu)

    out = jnp.concatenate([c, yb], axis=-1) @ w_out
    return x + rmsnorm(out, post_g), new_conv_buf, new_lconv_buf, h_last


def decoder_layer(x, p, conv_buf, lconv_buf, h0, l,
                  ffn1_pre_g, ffn1_post_g, ffn1_w_gu, ffn1_w_down,
                  mix_pre_g, mix_post_g, w_in, conv_w, conv_b, conv_norm_g,
                  lru_conv_w, lru_conv_b, lru_wa, lru_ba, lru_wx, lru_bx, lru_lambda, w_out,
                  ffn2_pre_g, ffn2_post_g, ffn2_w_gu, ffn2_w_down,
                  ple_norm_g, ple_w_gate, ple_w_proj, ple_post_g):
    x = x + FFN_RES_WEIGHT * rmsnorm(swiglu_ffn(rmsnorm(x, ffn1_pre_g[l]), ffn1_w_gu[l], ffn1_w_down[l]),
                                     ffn1_post_g[l])
    x, cb, lcb, hl = mixer_block(x, conv_buf, lconv_buf, h0, mix_pre_g[l], mix_post_g[l], w_in[l],
                                 conv_w[l], conv_b[l], conv_norm_g[l], lru_conv_w[l], lru_conv_b[l],
                                 lru_wa[l], lru_ba[l], lru_wx[l], lru_bx[l], lru_lambda[l], w_out[l])
    x = x + FFN_RES_WEIGHT * rmsnorm(swiglu_ffn(rmsnorm(x, ffn2_pre_g[l]), ffn2_w_gu[l], ffn2_w_down[l]),
                                     ffn2_post_g[l])
    gate = jax.nn.sigmoid(rmsnorm(x, ple_norm_g[l]) @ ple_w_gate[l])
    e = p @ ple_w_proj[l]
    x = x + rmsnorm(gate * e, ple_post_g[l])
    return x, cb, lcb, hl


def setup_inputs(seed: int = 0) -> dict:
    key = jax.random.key(seed)
    ks = iter(jax.random.split(key, 40))
    f32 = jnp.float32

    def nrm(shape, scale):
        return jax.random.normal(next(ks), shape, f32) * scale

    def gain(shape):
        return 1.0 + 0.05 * jax.random.normal(next(ks), shape, f32)

    x_prompt = nrm((BATCH, SEQ, D_MODEL), 1.0)
    x_sample = nrm((DEC_BATCH, DEC_SEQ, D_MODEL), 1.0)
    cache_conv = nrm((DEPTH, DEC_BATCH, CONV_WIDTH - 1, D_CONV), 0.5)
    state_lru_conv = nrm((DEPTH, DEC_BATCH, LRU_CONV_WIDTH - 1, D_LRU), 0.5)
    state_lru_h = nrm((DEPTH, DEC_BATCH, D_LRU), 0.5)
    p_prompt = nrm((DEPTH, BATCH, SEQ, D_PLE), 1.0)
    p_sample = nrm((DEPTH, DEC_BATCH, DEC_SEQ, D_PLE), 1.0)

    u = jax.random.uniform(next(ks), (DEPTH, D_LRU), f32, minval=0.9, maxval=0.999)
    a0 = u ** (1.0 / LRU_C)
    lru_lambda = jnp.log(a0) - jnp.log1p(-a0)

    return {
        "x_prompt": x_prompt, "x_sample": x_sample,
        "cache_conv": cache_conv, "state_lru_conv": state_lru_conv, "state_lru_h": state_lru_h,
        "p_prompt": p_prompt, "p_sample": p_sample,
        "ffn1_pre_g": gain((DEPTH, D_MODEL)), "ffn1_post_g": gain((DEPTH, D_MODEL)),
        "ffn1_w_gu": nrm((DEPTH, D_MODEL, 2 * D_FF), D_MODEL ** -0.5),
        "ffn1_w_down": nrm((DEPTH, D_FF, D_MODEL), D_FF ** -0.5),
        "mix_pre_g": gain((DEPTH, D_MODEL)), "mix_post_g": gain((DEPTH, D_MODEL)),
        "w_in": nrm((DEPTH, D_MODEL, D_IN), D_MODEL ** -0.5),
        "conv_w": nrm((DEPTH, CONV_WIDTH, D_CONV), CONV_WIDTH ** -0.5),
        "conv_b": nrm((DEPTH, D_CONV), 0.01),
        "conv_norm_g": gain((DEPTH, D_CONV)),
        "lru_conv_w": nrm((DEPTH, LRU_CONV_WIDTH, D_LRU), LRU_CONV_WIDTH ** -0.5),
        "lru_conv_b": nrm((DEPTH, D_LRU), 0.01),
        "lru_wa": nrm((DEPTH, LRU_HEADS, LRU_HEAD_DIM, LRU_HEAD_DIM), LRU_HEAD_DIM ** -0.5),
        "lru_ba": nrm((DEPTH, D_LRU), 0.01),
        "lru_wx": nrm((DEPTH, LRU_HEADS, LRU_HEAD_DIM, LRU_HEAD_DIM), LRU_HEAD_DIM ** -0.5),
        "lru_bx": nrm((DEPTH, D_LRU), 0.01),
        "lru_lambda": lru_lambda,
        "w_out": nrm((DEPTH, D_MIX, D_MODEL), D_MIX ** -0.5),
        "ffn2_pre_g": gain((DEPTH, D_MODEL)), "ffn2_post_g": gain((DEPTH, D_MODEL)),
        "ffn2_w_gu": nrm((DEPTH, D_MODEL, 2 * D_FF), D_MODEL ** -0.5),
        "ffn2_w_down": nrm((DEPTH, D_FF, D_MODEL), D_FF ** -0.5),
        "ple_norm_g": gain((DEPTH, D_MODEL)),
        "ple_w_gate": nrm((DEPTH, D_MODEL, D_MODEL), D_MODEL ** -0.5),
        "ple_w_proj": nrm((DEPTH, D_PLE, D_MODEL), D_PLE ** -0.5),
        "ple_post_g": gain((DEPTH, D_MODEL)),
    }


def reference(x_prompt, x_sample, cache_conv, state_lru_conv, state_lru_h, p_prompt, p_sample,
              ffn1_pre_g, ffn1_post_g, ffn1_w_gu, ffn1_w_down,
              mix_pre_g, mix_post_g, w_in, conv_w, conv_b, conv_norm_g,
              lru_conv_w, lru_conv_b, lru_wa, lru_ba, lru_wx, lru_bx, lru_lambda, w_out,
              ffn2_pre_g, ffn2_post_g, ffn2_w_gu, ffn2_w_down,
              ple_norm_g, ple_w_gate, ple_w_proj, ple_post_g):
    dt = x_prompt.dtype
    xp, xs = x_prompt, x_sample
    conv_p, lconv_p, h_p = [], [], []
    conv_s, lconv_s, h_s = [], [], []
    for l in range(DEPTH):
        zc = jnp.zeros((BATCH, CONV_WIDTH - 1, D_CONV), dt)
        zl = jnp.zeros((BATCH, LRU_CONV_WIDTH - 1, D_LRU), dt)
        zh = jnp.zeros((BATCH, D_LRU), state_lru_h.dtype)
        xp, cb, lcb, hl = decoder_layer(
            xp, p_prompt[l], zc, zl, zh, l,
            ffn1_pre_g, ffn1_post_g, ffn1_w_gu, ffn1_w_down,
            mix_pre_g, mix_post_g, w_in, conv_w, conv_b, conv_norm_g,
            lru_conv_w, lru_conv_b, lru_wa, lru_ba, lru_wx, lru_bx, lru_lambda, w_out,
            ffn2_pre_g, ffn2_post_g, ffn2_w_gu, ffn2_w_down,
            ple_norm_g, ple_w_gate, ple_w_proj, ple_post_g)
        conv_p.append(cb); lconv_p.append(lcb); h_p.append(hl)
        xs, cb, lcb, hl = decoder_layer(
            xs, p_sample[l], cache_conv[l], state_lru_conv[l], state_lru_h[l], l,
            ffn1_pre_g, ffn1_post_g, ffn1_w_gu, ffn1_w_down,
            mix_pre_g, mix_post_g, w_in, conv_w, conv_b, conv_norm_g,
            lru_conv_w, lru_conv_b, lru_wa, lru_ba, lru_wx, lru_bx, lru_lambda, w_out,
            ffn2_pre_g, ffn2_post_g, ffn2_w_gu, ffn2_w_down,
            ple_norm_g, ple_w_gate, ple_w_proj, ple_post_g)
        conv_s.append(cb); lconv_s.append(lcb); h_s.append(hl)
    new_conv_p = jnp.stack(conv_p)
    new_lru_conv_p = jnp.stack(lconv_p)
    new_lru_h_p = jnp.stack(h_p)
    new_conv_s = jnp.stack(conv_s)
    new_lru_conv_s = jnp.stack(lconv_s)
    new_lru_h_s = jnp.stack(h_s)
    return (xp, xs, new_conv_p, new_lru_conv_p, new_lru_h_p, new_conv_s, new_lru_conv_s, new_lru_h_s)
```

```python
import functools

import jax
import jax.numpy as jnp
from jax import lax
from jax.experimental import pallas as pl
from jax.experimental.pallas import tpu as pltpu

D_MODEL = 1024
D_CONV = 512
D_LRU = 512
D_IN = 2 * D_CONV + 2 * D_LRU
D_FF = 2816
D_PLE = 256
LRU_HEADS = 8
CONV_WIDTH = 31
LRU_CONV_WIDTH = 4
LRU_C = 8.0
EPS = 1e-6
FFN_RES_WEIGHT = 0.5

SUBLANES = 8
VMEM_LIMIT_BYTES = 56 * 1024 * 1024

TOKEN_TILE = 512
TIME_TILE = 512
CONV_ROWS = 32
CONV_PAD = 32
LCONV_PAD = 8
SAMPLE_BATCH_TILE = 32

FF_CHUNKS = tuple((lo, min(512, D_FF - lo)) for lo in range(0, D_FF, 512))

_bf16 = jnp.bfloat16
_f32 = jnp.float32


def _rms(x, g):
    ms = jnp.mean(x * x, axis=-1, keepdims=True)
    return x * lax.rsqrt(ms + EPS) * g


def _dot(a, b):
    return jnp.dot(a, b, preferred_element_type=_f32)


def _ffn(x, pre_g, post_g, wgu_ref, wd_ref, act_ref):
    h = _rms(x, pre_g).astype(_bf16)
    for lo, n in FF_CHUNKS:
        gate = _dot(h, wgu_ref[:, lo:lo + n])
        up = _dot(h, wgu_ref[:, D_FF + lo:D_FF + lo + n])
        act_ref[:, lo:lo + n] = (gate * jax.nn.sigmoid(gate) * up).astype(_bf16)
    y = _dot(act_ref[...], wd_ref[...])
    return x + FFN_RES_WEIGHT * _rms(y, post_g)


def _ffn_in_kernel(x_ref, pre_g, post_g, wgu, wd, mix_g, w_in, x1_ref, u_ref, act_ref):
    x1 = _ffn(x_ref[...], pre_g[...], post_g[...], wgu, wd, act_ref)
    x1_ref[...] = x1
    u_ref[...] = _dot(_rms(x1, mix_g[...]).astype(_bf16), w_in[...])


def _ffn_ple_kernel(x_ref, p_ref, pre_g, post_g, wgu, wd, ple_g, w_gate, w_proj, ple_post_g,
                    y_ref, act_ref):
    x3 = _ffn(x_ref[...], pre_g[...], post_g[...], wgu, wd, act_ref)
    gate = jax.nn.sigmoid(_dot(_rms(x3, ple_g[...]).astype(_bf16), w_gate[...]))
    e = _dot(p_ref[...].astype(_bf16), w_proj[...])
    y_ref[...] = x3 + _rms(gate * e, ple_post_g[...])


def _const_spec(shape):
    zeros = (0,) * len(shape)
    return pl.BlockSpec(shape, lambda *_: zeros, pipeline_mode=pl.Buffered(1))


def _row_spec(tile, width):
    return pl.BlockSpec((tile, width), lambda i: (i, 0))


def _ffn_in_call(x, pre_g, post_g, wgu, wd, mix_g, w_in):
    n = x.shape[0]
    tile = min(TOKEN_TILE, n)
    return pl.pallas_call(
        _ffn_in_kernel,
        grid=(n // tile,),
        in_specs=[_row_spec(tile, D_MODEL), _const_spec(pre_g.shape), _const_spec(post_g.shape),
                  _const_spec(wgu.shape), _const_spec(wd.shape), _const_spec(mix_g.shape),
                  _const_spec(w_in.shape)],
        out_specs=[_row_spec(tile, D_MODEL), _row_spec(tile, D_IN)],
        out_shape=[jax.ShapeDtypeStruct((n, D_MODEL), _f32), jax.ShapeDtypeStruct((n, D_IN), _f32)],
        scratch_shapes=[pltpu.VMEM((tile, D_FF), _bf16)],
        compiler_params=pltpu.CompilerParams(dimension_semantics=("arbitrary",),
                                             vmem_limit_bytes=VMEM_LIMIT_BYTES),
        name="ffn1_in_proj",
    )(x, pre_g, post_g, wgu, wd, mix_g, w_in)


def _ffn_ple_call(x, p, pre_g, post_g, wgu, wd, ple_g, w_gate, w_proj, ple_post_g):
    n = x.shape[0]
    tile = min(TOKEN_TILE, n)
    return pl.pallas_call(
        _ffn_ple_kernel,
        grid=(n // tile,),
        in_specs=[_row_spec(tile, D_MODEL), _row_spec(tile, D_PLE),
                  _const_spec(pre_g.shape), _const_spec(post_g.shape),
                  _const_spec(wgu.shape), _const_spec(wd.shape), _const_spec(ple_g.shape),
                  _const_spec(w_gate.shape), _const_spec(w_proj.shape), _const_spec(ple_post_g.shape)],
        out_specs=_row_spec(tile, D_MODEL),
        out_shape=jax.ShapeDtypeStruct((n, D_MODEL), _f32),
        scratch_shapes=[pltpu.VMEM((tile, D_FF), _bf16)],
        compiler_params=pltpu.CompilerParams(dimension_semantics=("arbitrary",),
                                             vmem_limit_bytes=VMEM_LIMIT_BYTES),
        name="ffn2_ple",
    )(x, p, pre_g, post_g, wgu, wd, ple_g, w_gate, w_proj, ple_post_g)


def _lru_coeffs(xc, w_gates, ba, bx, lam):
    gates = _dot(xc.astype(_bf16), w_gates)
    r = jax.nn.sigmoid(gates[:, :D_LRU] + ba)
    i_g = jax.nn.sigmoid(gates[:, D_LRU:] + bx)
    neg_lam = -lam
    softplus = jnp.maximum(neg_lam, 0.0) + jnp.log1p(jnp.exp(-jnp.abs(neg_lam)))
    log_a = -LRU_C * r * softplus
    a = jnp.exp(log_a)
    b = jnp.sqrt(-jnp.tanh(log_a) * (a * a + 1.0)) * (i_g * xc)
    return a, b


def _mix_prompt_kernel(u_ref, x1_ref, conv_w, conv_b, conv_g, lconv_w, lconv_b, w_gates, ba, bx, lam,
                       w_out, post_g,
                       x2_ref, conv_st_ref, lconv_st_ref, h_st_ref,
                       gbuf, xbuf, cbuf, hbuf, hcar):
    tt = TIME_TILE
    t_idx = pl.program_id(1)
    last_t = pl.num_programs(1) - 1

    @pl.when(t_idx == 0)
    def _():
        gbuf[0:CONV_PAD, :] = jnp.zeros((CONV_PAD, D_CONV), _f32)
        xbuf[0:LCONV_PAD, :] = jnp.zeros((LCONV_PAD, D_LRU), _f32)
        hcar[...] = jnp.zeros((1, D_LRU), _f32)

    gbuf[CONV_PAD:CONV_PAD + tt, :] = u_ref[0, :, 0:D_CONV] * jax.nn.sigmoid(u_ref[0, :, D_CONV:2 * D_CONV])
    first = CONV_PAD - (CONV_WIDTH - 1)
    for r0 in range(0, tt, CONV_ROWS):
        acc = jnp.broadcast_to(conv_b[...], (CONV_ROWS, D_CONV))
        for k in range(CONV_WIDTH):
            acc = acc + conv_w[k:k + 1, :] * gbuf[pl.ds(first + r0 + k, CONV_ROWS), :]
        cn = _rms(acc, conv_g[...])
        cbuf[r0:r0 + CONV_ROWS, :] = (cn * jax.nn.sigmoid(cn)).astype(_bf16)

    @pl.when(t_idx == last_t)
    def _():
        conv_st_ref[0] = gbuf[pl.ds(first + tt, CONV_WIDTH - 1), :]

    gbuf[0:CONV_PAD, :] = gbuf[tt:tt + CONV_PAD, :]

    xbuf[LCONV_PAD:LCONV_PAD + tt, :] = u_ref[0, :, 2 * D_CONV:2 * D_CONV + D_LRU]
    lfirst = LCONV_PAD - (LRU_CONV_WIDTH - 1)
    xc = jnp.broadcast_to(lconv_b[...], (tt, D_LRU))
    for k in range(LRU_CONV_WIDTH):
        xc = xc + lconv_w[k:k + 1, :] * xbuf[pl.ds(lfirst + k, tt), :]

    @pl.when(t_idx == last_t)
    def _():
        lconv_st_ref[0] = xbuf[pl.ds(lfirst + tt, LRU_CONV_WIDTH - 1), :]

    xbuf[0:LCONV_PAD, :] = xbuf[tt:tt + LCONV_PAD, :]

    a, b = _lru_coeffs(xc, w_gates[...], ba[...], bx[...], lam[...])

    row = lax.broadcasted_iota(jnp.int32, (tt, D_LRU), 0) % SUBLANES
    shift = 1
    while shift < SUBLANES:
        keep = row >= shift
        b = jnp.where(keep, a * pltpu.roll(b, shift, axis=0) + b, b)
        a = jnp.where(keep, a * pltpu.roll(a, shift, axis=0), a)
        shift *= 2
    carry = hcar[...]
    for r0 in range(0, tt, SUBLANES):
        rows = a[r0:r0 + SUBLANES, :] * carry + b[r0:r0 + SUBLANES, :]
        hbuf[r0:r0 + SUBLANES, :] = rows
        carry = rows[SUBLANES - 1:SUBLANES, :]
    hcar[...] = carry

    @pl.when(t_idx == last_t)
    def _():
        h_st_ref[0] = carry

    yb = (hbuf[...] * jax.nn.gelu(u_ref[0, :, 2 * D_CONV + D_LRU:D_IN])).astype(_bf16)
    out = _dot(cbuf[...], w_out[0:D_CONV, :]) + _dot(yb, w_out[D_CONV:D_CONV + D_LRU, :])
    x2_ref[0] = x1_ref[0] + _rms(out, post_g[...])


def _mix_prompt_call(u, x1, conv_w, conv_b, conv_g, lconv_w, lconv_b, w_gates, ba, bx, lam, w_out, post_g):
    bsz, t, _ = u.shape
    tt = TIME_TILE
    consts = (conv_w, conv_b, conv_g, lconv_w, lconv_b, w_gates, ba, bx, lam, w_out, post_g)
    return pl.pallas_call(
        _mix_prompt_kernel,
        grid=(bsz, t // tt),
        in_specs=[pl.BlockSpec((1, tt, D_IN), lambda b, i: (b, i, 0)),
                  pl.BlockSpec((1, tt, D_MODEL), lambda b, i: (b, i, 0))]
                 + [_const_spec(c.shape) for c in consts],
        out_specs=[pl.BlockSpec((1, tt, D_MODEL), lambda b, i: (b, i, 0)),
                   pl.BlockSpec((1, CONV_WIDTH - 1, D_CONV), lambda b, i: (b, 0, 0)),
                   pl.BlockSpec((1, LRU_CONV_WIDTH - 1, D_LRU), lambda b, i: (b, 0, 0)),
                   pl.BlockSpec((1, 1, D_LRU), lambda b, i: (b, 0, 0))],
        out_shape=[jax.ShapeDtypeStruct((bsz, t, D_MODEL), _f32),
                   jax.ShapeDtypeStruct((bsz, CONV_WIDTH - 1, D_CONV), _f32),
                   jax.ShapeDtypeStruct((bsz, LRU_CONV_WIDTH - 1, D_LRU), _f32),
                   jax.ShapeDtypeStruct((bsz, 1, D_LRU), _f32)],
        scratch_shapes=[pltpu.VMEM((CONV_PAD + tt, D_CONV), _f32),
                        pltpu.VMEM((LCONV_PAD + tt, D_LRU), _f32),
                        pltpu.VMEM((tt, D_CONV), _bf16),
                        pltpu.VMEM((tt, D_LRU), _f32),
                        pltpu.VMEM((1, D_LRU), _f32)],
        compiler_params=pltpu.CompilerParams(dimension_semantics=("arbitrary", "arbitrary"),
                                             vmem_limit_bytes=VMEM_LIMIT_BYTES),
        name="mix_prompt",
    )(u, x1, *consts)


def _mix_sample_kernel(u_ref, x1_ref, cache_ref, lstate_ref, h0_ref,
                       conv_w, conv_b, conv_g, lconv_w, lconv_b, w_gates, ba, bx, lam, w_out, post_g,
                       x2_ref, conv_st_ref, lconv_st_ref, h_st_ref):
    steps = u_ref.shape[0]
    past = CONV_WIDTH - 1
    lpast = LRU_CONV_WIDTH - 1

    gp = [cache_ref[:, j, :] for j in range(past)]
    gp += [u_ref[t, :, 0:D_CONV] * jax.nn.sigmoid(u_ref[t, :, D_CONV:2 * D_CONV]) for t in range(steps)]
    for j in range(past):
        conv_st_ref[:, j, :] = gp[j + steps]
    c_rows = []
    for t in range(steps):
        acc = jnp.broadcast_to(conv_b[...], gp[0].shape)
        for k in range(CONV_WIDTH):
            acc = acc + conv_w[k:k + 1, :] * gp[t + k]
        cn = _rms(acc, conv_g[...])
        c_rows.append(cn * jax.nn.sigmoid(cn))
    c = jnp.concatenate(c_rows, axis=0).astype(_bf16)

    xp = [lstate_ref[:, j, :] for j in range(lpast)]
    xp += [u_ref[t, :, 2 * D_CONV:2 * D_CONV + D_LRU] for t in range(steps)]
    for j in range(lpast):
        lconv_st_ref[:, j, :] = xp[j + steps]
    xc_rows = []
    for t in range(steps):
        acc = jnp.broadcast_to(lconv_b[...], xp[0].shape)
        for k in range(LRU_CONV_WIDTH):
            acc = acc + lconv_w[k:k + 1, :] * xp[t + k]
        xc_rows.append(acc)
    xc = jnp.concatenate(xc_rows, axis=0)
    a, b = _lru_coeffs(xc, w_gates[...], ba[...], bx[...], lam[...])
    bt = h0_ref.shape[0]
    h = h0_ref[...]
    hs = []
    for t in range(steps):
        h = a[t * bt:(t + 1) * bt, :] * h + b[t * bt:(t + 1) * bt, :]
        hs.append(h)
    h_st_ref[...] = h
    u_gelu = jnp.concatenate([u_ref[t, :, 2 * D_CONV + D_LRU:D_IN] for t in range(steps)], axis=0)
    yb = (jnp.concatenate(hs, axis=0) * jax.nn.gelu(u_gelu)).astype(_bf16)

    out = _dot(c, w_out[0:D_CONV, :]) + _dot(yb, w_out[D_CONV:D_CONV + D_LRU, :])
    x2 = _rms(out, post_g[...])
    for t in range(steps):
        x2_ref[t] = x1_ref[t] + x2[t * bt:(t + 1) * bt, :]


def _mix_sample_call(u, x1, cache, lstate, h0, conv_w, conv_b, conv_g, lconv_w, lconv_b, w_gates, ba, bx,
                     lam, w_out, post_g):
    steps, bsz, _ = u.shape
    bt = SAMPLE_BATCH_TILE
    consts = (conv_w, conv_b, conv_g, lconv_w, lconv_b, w_gates, ba, bx, lam, w_out, post_g)
    return pl.pallas_call(
        _mix_sample_kernel,
        grid=(bsz // bt,),
        in_specs=[pl.BlockSpec((steps, bt, D_IN), lambda i: (0, i, 0)),
                  pl.BlockSpec((steps, bt, D_MODEL), lambda i: (0, i, 0)),
                  pl.BlockSpec((bt, CONV_WIDTH - 1, D_CONV), lambda i: (i, 0, 0)),
                  pl.BlockSpec((bt, LRU_CONV_WIDTH - 1, D_LRU), lambda i: (i, 0, 0)),
                  pl.BlockSpec((bt, D_LRU), lambda i: (i, 0))]
                 + [_const_spec(c.shape) for c in consts],
        out_specs=[pl.BlockSpec((steps, bt, D_MODEL), lambda i: (0, i, 0)),
                   pl.BlockSpec((bt, CONV_WIDTH - 1, D_CONV), lambda i: (i, 0, 0)),
                   pl.BlockSpec((bt, LRU_CONV_WIDTH - 1, D_LRU), lambda i: (i, 0, 0)),
                   pl.BlockSpec((bt, D_LRU), lambda i: (i, 0))],
        out_shape=[jax.ShapeDtypeStruct((steps, bsz, D_MODEL), _f32),
                   jax.ShapeDtypeStruct((bsz, CONV_WIDTH - 1, D_CONV), _f32),
                   jax.ShapeDtypeStruct((bsz, LRU_CONV_WIDTH - 1, D_LRU), _f32),
                   jax.ShapeDtypeStruct((bsz, D_LRU), _f32)],
        compiler_params=pltpu.CompilerParams(dimension_semantics=("arbitrary",),
                                             vmem_limit_bytes=VMEM_LIMIT_BYTES),
        name="mix_sample",
    )(u, x1, cache, lstate, h0, *consts)


def _block_diag(w):
    heads, d, _ = w.shape
    eye = jnp.eye(heads, dtype=w.dtype)
    return (eye[:, None, :, None] * w[:, :, None, :]).reshape(heads * d, heads * d)


def kernel(x_prompt, x_sample, cache_conv, state_lru_conv, state_lru_h, p_prompt, p_sample, ffn1_pre_g, ffn1_post_g, ffn1_w_gu, ffn1_w_down, mix_pre_g, mix_post_g, w_in, conv_w, conv_b, conv_norm_g, lru_conv_w, lru_conv_b, lru_wa, lru_ba, lru_wx, lru_bx, lru_lambda, w_out, ffn2_pre_g, ffn2_post_g, ffn2_w_gu, ffn2_w_down, ple_norm_g, ple_w_gate, ple_w_proj, ple_post_g):
    depth = ffn1_w_gu.shape[0]
    bsz, seq, _ = x_prompt.shape
    dbsz, dseq, _ = x_sample.shape

    xp = x_prompt.reshape(bsz * seq, D_MODEL)
    xs = x_sample.transpose(1, 0, 2).reshape(dseq * dbsz, D_MODEL)
    outs = [[] for _ in range(6)]
    for l in range(depth):
        row = lambda v: v[l][None, :]
        wgu1, wd1 = ffn1_w_gu[l].astype(_bf16), ffn1_w_down[l].astype(_bf16)
        wgu2, wd2 = ffn2_w_gu[l].astype(_bf16), ffn2_w_down[l].astype(_bf16)
        w_in_l, w_out_l = w_in[l].astype(_bf16), w_out[l].astype(_bf16)
        w_gate_l, w_proj_l = ple_w_gate[l].astype(_bf16), ple_w_proj[l].astype(_bf16)
        w_gates = jnp.concatenate([_block_diag(lru_wa[l]), _block_diag(lru_wx[l])], axis=1).astype(_bf16)
        mix_consts = (conv_w[l], row(conv_b), row(conv_norm_g), lru_conv_w[l], row(lru_conv_b), w_gates,
                      row(lru_ba), row(lru_bx), row(lru_lambda), w_out_l, row(mix_post_g))
        ffn1_args = (row(ffn1_pre_g), row(ffn1_post_g), wgu1, wd1, row(mix_pre_g), w_in_l)
        ffn2_args = (row(ffn2_pre_g), row(ffn2_post_g), wgu2, wd2, row(ple_norm_g), w_gate_l, w_proj_l,
                     row(ple_post_g))

        x1, u = _ffn_in_call(xp, *ffn1_args)
        x2, cst, lst, hst = _mix_prompt_call(u.reshape(bsz, seq, D_IN), x1.reshape(bsz, seq, D_MODEL),
                                             *mix_consts)
        xp = _ffn_ple_call(x2.reshape(bsz * seq, D_MODEL), p_prompt[l].reshape(bsz * seq, D_PLE), *ffn2_args)
        outs[0].append(cst); outs[1].append(lst); outs[2].append(hst.reshape(bsz, D_LRU))

        x1, u = _ffn_in_call(xs, *ffn1_args)
        x2, cst, lst, hst = _mix_sample_call(u.reshape(dseq, dbsz, D_IN), x1.reshape(dseq, dbsz, D_MODEL),
                                             cache_conv[l], state_lru_conv[l], state_lru_h[l], *mix_consts)
        ps = p_sample[l].transpose(1, 0, 2).reshape(dseq * dbsz, D_PLE)
        xs = _ffn_ple_call(x2.reshape(dseq * dbsz, D_MODEL), ps, *ffn2_args)
        outs[3].append(cst); outs[4].append(lst); outs[5].append(hst)

    y_prompt = xp.reshape(bsz, seq, D_MODEL)
    y_sample = xs.reshape(dseq, dbsz, D_MODEL).transpose(1, 0, 2)
    return (y_prompt, y_sample) + tuple(jnp.stack(o) for o in outs)
```

```python
import functools

import jax
import jax.numpy as jnp
from jax import lax
from jax.experimental import pallas as pl
from jax.experimental.pallas import tpu as pltpu

D_MODEL = 1024
D_CONV = 512
D_LRU = 512
D_IN = 2 * D_CONV + 2 * D_LRU
D_FF = 2816
D_PLE = 256
LRU_HEADS = 8
CONV_WIDTH = 31
LRU_CONV_WIDTH = 4
LRU_C = 8.0
EPS = 1e-6
FFN_RES_WEIGHT = 0.5

SUBLANES = 8
LANES = 128
VMEM_LIMIT_BYTES = 56 * 1024 * 1024

TOKEN_TILE = 512
TIME_TILE = 256
CONV_ROWS = 64
CONV_PAD = 32
LCONV_PAD = 8
SAMPLE_BATCH_TILE = 32

FF_CHUNKS = tuple((lo, min(512, D_FF - lo)) for lo in range(0, D_FF, 512))

_bf16 = jnp.bfloat16
_f32 = jnp.float32


def _rms(x, g):
    ms = jnp.mean(x * x, axis=-1, keepdims=True)
    return x * lax.rsqrt(ms + EPS) * g


def _dot(a, b):
    return jnp.dot(a, b, preferred_element_type=_f32)


def _ffn(x, pre_g, post_g, wgu_ref, wd_ref, act_ref):
    h = _rms(x, pre_g).astype(_bf16)
    for lo, n in FF_CHUNKS:
        gate = _dot(h, wgu_ref[:, lo:lo + n])
        up = _dot(h, wgu_ref[:, D_FF + lo:D_FF + lo + n])
        act_ref[:, lo:lo + n] = (gate * jax.nn.sigmoid(gate) * up).astype(_bf16)
    y = _dot(act_ref[...], wd_ref[...])
    return x + FFN_RES_WEIGHT * _rms(y, post_g)


def _ffn_in_kernel(x_ref, pre_g, post_g, wgu, wd, mix_g, w_in, x1_ref, u_ref, act_ref):
    x1 = _ffn(x_ref[...], pre_g[...], post_g[...], wgu, wd, act_ref)
    x1_ref[...] = x1
    u_ref[...] = _dot(_rms(x1, mix_g[...]).astype(_bf16), w_in[...])


def _ffn_ple_kernel(x_ref, p_ref, pre_g, post_g, wgu, wd, ple_g, w_gate, w_proj, ple_post_g,
                    y_ref, act_ref):
    x3 = _ffn(x_ref[...], pre_g[...], post_g[...], wgu, wd, act_ref)
    gate = jax.nn.sigmoid(_dot(_rms(x3, ple_g[...]).astype(_bf16), w_gate[...]))
    e = _dot(p_ref[...].astype(_bf16), w_proj[...])
    y_ref[...] = x3 + _rms(gate * e, ple_post_g[...])


def _const_spec(shape):
    zeros = (0,) * len(shape)
    return pl.BlockSpec(shape, lambda *_: zeros, pipeline_mode=pl.Buffered(1))


def _row_spec(tile, width):
    return pl.BlockSpec((tile, width), lambda i: (i, 0))


def _ffn_in_call(x, pre_g, post_g, wgu, wd, mix_g, w_in):
    n = x.shape[0]
    tile = min(TOKEN_TILE, n)
    return pl.pallas_call(
        _ffn_in_kernel,
        grid=(n // tile,),
        in_specs=[_row_spec(tile, D_MODEL), _const_spec(pre_g.shape), _const_spec(post_g.shape),
                  _const_spec(wgu.shape), _const_spec(wd.shape), _const_spec(mix_g.shape),
                  _const_spec(w_in.shape)],
        out_specs=[_row_spec(tile, D_MODEL), _row_spec(tile, D_IN)],
        out_shape=[jax.ShapeDtypeStruct((n, D_MODEL), _f32), jax.ShapeDtypeStruct((n, D_IN), _f32)],
        scratch_shapes=[pltpu.VMEM((tile, D_FF), _bf16)],
        compiler_params=pltpu.CompilerParams(dimension_semantics=("arbitrary",),
                                             vmem_limit_bytes=VMEM_LIMIT_BYTES),
        name="ffn1_in_proj",
    )(x, pre_g, post_g, wgu, wd, mix_g, w_in)


def _ffn_ple_call(x, p, pre_g, post_g, wgu, wd, ple_g, w_gate, w_proj, ple_post_g):
    n = x.shape[0]
    tile = min(TOKEN_TILE, n)
    return pl.pallas_call(
        _ffn_ple_kernel,
        grid=(n // tile,),
        in_specs=[_row_spec(tile, D_MODEL), _row_spec(tile, D_PLE),
                  _const_spec(pre_g.shape), _const_spec(post_g.shape),
                  _const_spec(wgu.shape), _const_spec(wd.shape), _const_spec(ple_g.shape),
                  _const_spec(w_gate.shape), _const_spec(w_proj.shape), _const_spec(ple_post_g.shape)],
        out_specs=_row_spec(tile, D_MODEL),
        out_shape=jax.ShapeDtypeStruct((n, D_MODEL), _f32),
        scratch_shapes=[pltpu.VMEM((tile, D_FF), _bf16)],
        compiler_params=pltpu.CompilerParams(dimension_semantics=("arbitrary",),
                                             vmem_limit_bytes=VMEM_LIMIT_BYTES),
        name="ffn2_ple",
    )(x, p, pre_g, post_g, wgu, wd, ple_g, w_gate, w_proj, ple_post_g)


def _lru_coeffs(xc, w_gates, ba, bx, lam):
    gates = _dot(xc.astype(_bf16), w_gates)
    r = jax.nn.sigmoid(gates[:, :D_LRU] + ba)
    i_g = jax.nn.sigmoid(gates[:, D_LRU:] + bx)
    neg_lam = -lam
    softplus = jnp.maximum(neg_lam, 0.0) + jnp.log1p(jnp.exp(-jnp.abs(neg_lam)))
    log_a = -LRU_C * r * softplus
    a = jnp.exp(log_a)
    b = jnp.sqrt(-jnp.tanh(log_a) * (a * a + 1.0)) * (i_g * xc)
    return a, b


def _causal_taps(buf, r0, c0, rows, pad, width, w_ref, b_ref):
    first = pad - (width - 1)
    x = buf[r0:r0 + rows + pad, c0:c0 + LANES]
    acc = jnp.broadcast_to(b_ref[:, c0:c0 + LANES], (rows, LANES))
    for phase in range(SUBLANES):
        taps = [(q, q * SUBLANES + phase - first) for q in range(pad // SUBLANES + 1)]
        taps = [(q, k) for q, k in taps if 0 <= k < width]
        if not taps:
            continue
        xs = x if phase == 0 else pltpu.roll(x, rows + pad - phase, axis=0)
        for q, k in taps:
            acc = acc + w_ref[k:k + 1, c0:c0 + LANES] * xs[q * SUBLANES:q * SUBLANES + rows, :]
    return acc


def _mix_head(ubuf, x1buf, fresh, x2_ref, gbuf, xbuf, gelubuf):
    tt = ubuf.shape[0]
    x2_ref[...] = x1buf[...]
    gbuf[0:CONV_PAD, :] = jnp.where(fresh, 0.0, gbuf[tt:tt + CONV_PAD, :])
    gbuf[CONV_PAD:CONV_PAD + tt, :] = ubuf[:, 0:D_CONV] * jax.nn.sigmoid(ubuf[:, D_CONV:2 * D_CONV])
    xbuf[0:LCONV_PAD, :] = jnp.where(fresh, 0.0, xbuf[tt:tt + LCONV_PAD, :])
    xbuf[LCONV_PAD:LCONV_PAD + tt, :] = ubuf[:, 2 * D_CONV:2 * D_CONV + D_LRU]
    gelubuf[...] = jax.nn.gelu(ubuf[:, 2 * D_CONV + D_LRU:D_IN])


def _mix_steps(fresh, conv_w, conv_b, conv_g, lconv_w, lconv_b, w_gates, ba, bx, lam, w_out, post_g,
               x2_ref, conv_st_ref, lconv_st_ref, h_st_ref, gbuf, xbuf, gelubuf, cbuf, xcbuf, hbuf, hcar):
    tt = x2_ref.shape[0]

    for r0 in range(0, tt, CONV_ROWS):
        for c0 in range(0, D_LRU, LANES):
            xcbuf[r0:r0 + CONV_ROWS, c0:c0 + LANES] = _causal_taps(
                xbuf, r0, c0, CONV_ROWS, LCONV_PAD, LRU_CONV_WIDTH, lconv_w, lconv_b)
    lconv_st_ref[0] = xbuf[pl.ds(LCONV_PAD + tt - (LRU_CONV_WIDTH - 1), LRU_CONV_WIDTH - 1), :]
    xc = xcbuf[...]
    yield xc

    a, b = _lru_coeffs(xc, w_gates[...], ba[...], bx[...], lam[...])
    yield b

    a = a.reshape(tt // SUBLANES, SUBLANES, D_LRU)
    b = b.reshape(tt // SUBLANES, SUBLANES, D_LRU)
    row = lax.broadcasted_iota(jnp.int32, a.shape, 1)
    shift = 1
    while shift < SUBLANES:
        keep = row >= shift
        b = jnp.where(keep, a * pltpu.roll(b, shift, axis=1) + b, b)
        a = jnp.where(keep, a * pltpu.roll(a, shift, axis=1), a)
        shift *= 2
    yield b[0]

    carry = jnp.where(fresh, 0.0, hcar[...])
    for i in range(tt // SUBLANES):
        rows = a[i] * carry + b[i]
        hbuf[i * SUBLANES:(i + 1) * SUBLANES, :] = rows
        carry = rows[SUBLANES - 1:SUBLANES, :]
    hcar[...] = carry
    h_st_ref[0] = carry
    yield rows

    for r0 in range(0, tt, CONV_ROWS):
        cols = [_causal_taps(gbuf, r0, c0, CONV_ROWS, CONV_PAD, CONV_WIDTH, conv_w, conv_b)
                for c0 in range(0, D_CONV, LANES)]
        cn = _rms(jnp.concatenate(cols, axis=1), conv_g[...])
        c = cn * jax.nn.sigmoid(cn)
        cbuf[r0:r0 + CONV_ROWS, :] = c.astype(_bf16)
        yield c
    conv_st_ref[0] = gbuf[pl.ds(CONV_PAD + tt - (CONV_WIDTH - 1), CONV_WIDTH - 1), :]

    yb = (hbuf[...] * gelubuf[...]).astype(_bf16)
    out = _dot(cbuf[...], w_out[0:D_CONV, :]) + _dot(yb, w_out[D_CONV:D_CONV + D_LRU, :])
    delta = _rms(out, post_g[...])
    x2_ref[...] += delta
    yield delta


ANCHOR_ROWS = 16


def _anchor(piece, width):
    if piece is None:
        return None
    tail = piece[piece.shape[0] - SUBLANES:, 0:LANES]
    zero = jnp.minimum(jnp.abs(tail), 0.0)
    zero = jnp.concatenate([zero] * (ANCHOR_ROWS // SUBLANES), axis=0)
    return jnp.concatenate([zero] * (width // LANES), axis=1)


def _anchored(value, zero):
    if zero is None:
        return value
    return jnp.concatenate([value[0:ANCHOR_ROWS] + zero, value[ANCHOR_ROWS:]], axis=0)


def _ffn_in_steps(mix, x_ref, pre_g, post_g, wgu_ref, wd_ref, mix_g, w_in, act_ref, x1buf, ubuf):
    x = x_ref[...]
    h = _rms(x, pre_g[...]).astype(_bf16)
    for lo, n in FF_CHUNKS:
        piece = next(mix, None)
        gate = _dot(h, wgu_ref[:, lo:lo + n])
        up = _dot(h, wgu_ref[:, D_FF + lo:D_FF + lo + n])
        act = _anchored(gate * jax.nn.sigmoid(gate) * up, _anchor(piece, n))
        act_ref[:, lo:lo + n] = act.astype(_bf16)
    half = D_MODEL // 2
    ys = []
    for lo in (0, half):
        piece = next(mix, None)
        ys.append(_anchored(_dot(act_ref[...], wd_ref[:, lo:lo + half]), _anchor(piece, half)))
    x1 = x + FFN_RES_WEIGHT * _rms(jnp.concatenate(ys, axis=1), post_g[...])
    x1buf[...] = x1
    hm = _rms(x1, mix_g[...]).astype(_bf16)
    half = D_IN // 2
    for lo in (0, half):
        piece = next(mix, None)
        ubuf[:, lo:lo + half] = _anchored(_dot(hm, w_in[:, lo:lo + half]), _anchor(piece, half))
    assert next(mix, None) is None


def _prompt_a_kernel(x_ref, pre_g, post_g, wgu, wd, mix_g, w_in,
                     conv_w, conv_b, conv_g, lconv_w, lconv_b, w_gates, ba, bx, lam, w_out, mix_post_g,
                     x2_ref, conv_st_ref, lconv_st_ref, h_st_ref,
                     act_ref, ubuf, x1buf, gbuf, xbuf, gelubuf, cbuf, xcbuf, hbuf, hcar, *, tiles_per_seq):
    s = pl.program_id(0)

    @pl.when(s == 0)
    def _():
        ubuf[...] = jnp.zeros(ubuf.shape, _f32)
        x1buf[...] = jnp.zeros(x1buf.shape, _f32)
        gbuf[...] = jnp.zeros(gbuf.shape, _f32)
        xbuf[...] = jnp.zeros(xbuf.shape, _f32)
        hcar[...] = jnp.zeros(hcar.shape, _f32)

    fresh = lax.rem(s + tiles_per_seq - 1, tiles_per_seq) == 0
    _mix_head(ubuf, x1buf, fresh, x2_ref, gbuf, xbuf, gelubuf)
    mix = _mix_steps(fresh, conv_w, conv_b, conv_g, lconv_w, lconv_b, w_gates, ba, bx, lam, w_out, mix_post_g,
                     x2_ref, conv_st_ref, lconv_st_ref, h_st_ref, gbuf, xbuf, gelubuf, cbuf, xcbuf, hbuf, hcar)
    _ffn_in_steps(mix, x_ref, pre_g, post_g, wgu, wd, mix_g, w_in, act_ref, x1buf, ubuf)


def _prompt_a_call(x, bsz, pre_g, post_g, wgu, wd, mix_g, w_in, conv_w, conv_b, conv_g, lconv_w, lconv_b,
                   w_gates, ba, bx, lam, w_out, mix_post_g):
    n = x.shape[0]
    tt = TIME_TILE
    tiles = n // tt
    tiles_per_seq = tiles // bsz
    consts = (pre_g, post_g, wgu, wd, mix_g, w_in, conv_w, conv_b, conv_g, lconv_w, lconv_b, w_gates, ba, bx,
              lam, w_out, mix_post_g)
    seq_of = lambda s: (jnp.maximum(s - 1, 0) // tiles_per_seq, 0, 0)
    return pl.pallas_call(
        functools.partial(_prompt_a_kernel, tiles_per_seq=tiles_per_seq),
        grid=(tiles + 1,),
        in_specs=[pl.BlockSpec((tt, D_MODEL), lambda s: (jnp.minimum(s, tiles - 1), 0))]
                 + [_const_spec(c.shape) for c in consts],
        out_specs=[pl.BlockSpec((tt, D_MODEL), lambda s: (jnp.maximum(s - 1, 0), 0)),
                   pl.BlockSpec((1, CONV_WIDTH - 1, D_CONV), seq_of),
                   pl.BlockSpec((1, LRU_CONV_WIDTH - 1, D_LRU), seq_of),
                   pl.BlockSpec((1, 1, D_LRU), seq_of)],
        out_shape=[jax.ShapeDtypeStruct((n, D_MODEL), _f32),
                   jax.ShapeDtypeStruct((bsz, CONV_WIDTH - 1, D_CONV), _f32),
                   jax.ShapeDtypeStruct((bsz, LRU_CONV_WIDTH - 1, D_LRU), _f32),
                   jax.ShapeDtypeStruct((bsz, 1, D_LRU), _f32)],
        scratch_shapes=[pltpu.VMEM((tt, D_FF), _bf16),
                        pltpu.VMEM((tt, D_IN), _f32),
                        pltpu.VMEM((tt, D_MODEL), _f32),
                        pltpu.VMEM((CONV_PAD + tt, D_CONV), _f32),
                        pltpu.VMEM((LCONV_PAD + tt, D_LRU), _f32),
                        pltpu.VMEM((tt, D_LRU), _f32),
                        pltpu.VMEM((tt, D_CONV), _bf16),
                        pltpu.VMEM((tt, D_LRU), _f32),
                        pltpu.VMEM((tt, D_LRU), _f32),
                        pltpu.VMEM((1, D_LRU), _f32)],
        compiler_params=pltpu.CompilerParams(dimension_semantics=("arbitrary",),
                                             vmem_limit_bytes=VMEM_LIMIT_BYTES),
        name="prompt_ffn1_mix",
    )(x, *consts)


def _mix_sample_kernel(u_ref, x1_ref, cache_ref, lstate_ref, h0_ref,
                       conv_w, conv_b, conv_g, lconv_w, lconv_b, w_gates, ba, bx, lam, w_out, post_g,
                       x2_ref, conv_st_ref, lconv_st_ref, h_st_ref):
    steps = u_ref.shape[0]
    past = CONV_WIDTH - 1
    lpast = LRU_CONV_WIDTH - 1

    gp = [cache_ref[:, j, :] for j in range(past)]
    gp += [u_ref[t, :, 0:D_CONV] * jax.nn.sigmoid(u_ref[t, :, D_CONV:2 * D_CONV]) for t in range(steps)]
    for j in range(past):
        conv_st_ref[:, j, :] = gp[j + steps]
    c_rows = []
    for t in range(steps):
        acc = jnp.broadcast_to(conv_b[...], gp[0].shape)
        for k in range(CONV_WIDTH):
            acc = acc + conv_w[k:k + 1, :] * gp[t + k]
        cn = _rms(acc, conv_g[...])
        c_rows.append(cn * jax.nn.sigmoid(cn))
    c = jnp.concatenate(c_rows, axis=0).astype(_bf16)

    xp = [lstate_ref[:, j, :] for j in range(lpast)]
    xp += [u_ref[t, :, 2 * D_CONV:2 * D_CONV + D_LRU] for t in range(steps)]
    for j in range(lpast):
        lconv_st_ref[:, j, :] = xp[j + steps]
    xc_rows = []
    for t in range(steps):
        acc = jnp.broadcast_to(lconv_b[...], xp[0].shape)
        for k in range(LRU_CONV_WIDTH):
            acc = acc + lconv_w[k:k + 1, :] * xp[t + k]
        xc_rows.append(acc)
    xc = jnp.concatenate(xc_rows, axis=0)
    a, b = _lru_coeffs(xc, w_gates[...], ba[...], bx[...], lam[...])
    bt = h0_ref.shape[0]
    h = h0_ref[...]
    hs = []
    for t in range(steps):
        h = a[t * bt:(t + 1) * bt, :] * h + b[t * bt:(t + 1) * bt, :]
        hs.append(h)
    h_st_ref[...] = h
    u_gelu = jnp.concatenate([u_ref[t, :, 2 * D_CONV + D_LRU:D_IN] for t in range(steps)], axis=0)
    yb = (jnp.concatenate(hs, axis=0) * jax.nn.gelu(u_gelu)).astype(_bf16)

    out = _dot(c, w_out[0:D_CONV, :]) + _dot(yb, w_out[D_CONV:D_CONV + D_LRU, :])
    x2 = _rms(out, post_g[...])
    for t in range(steps):
        x2_ref[t] = x1_ref[t] + x2[t * bt:(t + 1) * bt, :]


def _mix_sample_call(u, x1, cache, lstate, h0, conv_w, conv_b, conv_g, lconv_w, lconv_b, w_gates, ba, bx,
                     lam, w_out, post_g):
    steps, bsz, _ = u.shape
    bt = SAMPLE_BATCH_TILE
    consts = (conv_w, conv_b, conv_g, lconv_w, lconv_b, w_gates, ba, bx, lam, w_out, post_g)
    return pl.pallas_call(
        _mix_sample_kernel,
        grid=(bsz // bt,),
        in_specs=[pl.BlockSpec((steps, bt, D_IN), lambda i: (0, i, 0)),
                  pl.BlockSpec((steps, bt, D_MODEL), lambda i: (0, i, 0)),
                  pl.BlockSpec((bt, CONV_WIDTH - 1, D_CONV), lambda i: (i, 0, 0)),
                  pl.BlockSpec((bt, LRU_CONV_WIDTH - 1, D_LRU), lambda i: (i, 0, 0)),
                  pl.BlockSpec((bt, D_LRU), lambda i: (i, 0))]
                 + [_const_spec(c.shape) for c in consts],
        out_specs=[pl.BlockSpec((steps, bt, D_MODEL), lambda i: (0, i, 0)),
                   pl.BlockSpec((bt, CONV_WIDTH - 1, D_CONV), lambda i: (i, 0, 0)),
                   pl.BlockSpec((bt, LRU_CONV_WIDTH - 1, D_LRU), lambda i: (i, 0, 0)),
                   pl.BlockSpec((bt, D_LRU), lambda i: (i, 0))],
        out_shape=[jax.ShapeDtypeStruct((steps, bsz, D_MODEL), _f32),
                   jax.ShapeDtypeStruct((bsz, CONV_WIDTH - 1, D_CONV), _f32),
                   jax.ShapeDtypeStruct((bsz, LRU_CONV_WIDTH - 1, D_LRU), _f32),
                   jax.ShapeDtypeStruct((bsz, D_LRU), _f32)],
        compiler_params=pltpu.CompilerParams(dimension_semantics=("arbitrary",),
                                             vmem_limit_bytes=VMEM_LIMIT_BYTES),
        name="mix_sample",
    )(u, x1, cache, lstate, h0, *consts)


def _block_diag(w):
    heads, d, _ = w.shape
    eye = jnp.eye(heads, dtype=w.dtype)
    return (eye[:, None, :, None] * w[:, :, None, :]).reshape(heads * d, heads * d)


def kernel(x_prompt, x_sample, cache_conv, state_lru_conv, state_lru_h, p_prompt, p_sample, ffn1_pre_g, ffn1_post_g, ffn1_w_gu, ffn1_w_down, mix_pre_g, mix_post_g, w_in, conv_w, conv_b, conv_norm_g, lru_conv_w, lru_conv_b, lru_wa, lru_ba, lru_wx, lru_bx, lru_lambda, w_out, ffn2_pre_g, ffn2_post_g, ffn2_w_gu, ffn2_w_down, ple_norm_g, ple_w_gate, ple_w_proj, ple_post_g):
    depth = ffn1_w_gu.shape[0]
    bsz, seq, _ = x_prompt.shape
    dbsz, dseq, _ = x_sample.shape

    xp = x_prompt.reshape(bsz * seq, D_MODEL)
    xs = x_sample.transpose(1, 0, 2).reshape(dseq * dbsz, D_MODEL)
    outs = [[] for _ in range(6)]
    for l in range(depth):
        row = lambda v: v[l][None, :]
        wgu1, wd1 = ffn1_w_gu[l].astype(_bf16), ffn1_w_down[l].astype(_bf16)
        wgu2, wd2 = ffn2_w_gu[l].astype(_bf16), ffn2_w_down[l].astype(_bf16)
        w_in_l, w_out_l = w_in[l].astype(_bf16), w_out[l].astype(_bf16)
        w_gate_l, w_proj_l = ple_w_gate[l].astype(_bf16), ple_w_proj[l].astype(_bf16)
        w_gates = jnp.concatenate([_block_diag(lru_wa[l]), _block_diag(lru_wx[l])], axis=1).astype(_bf16)
        mix_consts = (conv_w[l], row(conv_b), row(conv_norm_g), lru_conv_w[l], row(lru_conv_b), w_gates,
                      row(lru_ba), row(lru_bx), row(lru_lambda), w_out_l, row(mix_post_g))
        ffn1_args = (row(ffn1_pre_g), row(ffn1_post_g), wgu1, wd1, row(mix_pre_g), w_in_l)
        ffn2_args = (row(ffn2_pre_g), row(ffn2_post_g), wgu2, wd2, row(ple_norm_g), w_gate_l, w_proj_l,
                     row(ple_post_g))

        x2, cst, lst, hst = _prompt_a_call(xp, bsz, *ffn1_args, *mix_consts)
        xp = _ffn_ple_call(x2, p_prompt[l].reshape(bsz * seq, D_PLE), *ffn2_args)
        outs[0].append(cst); outs[1].append(lst); outs[2].append(hst.reshape(bsz, D_LRU))

        x1, u = _ffn_in_call(xs, *ffn1_args)
        x2, cst, lst, hst = _mix_sample_call(u.reshape(dseq, dbsz, D_IN), x1.reshape(dseq, dbsz, D_MODEL),
                                             cache_conv[l], state_lru_conv[l], state_lru_h[l], *mix_consts)
        ps = p_sample[l].transpose(1, 0, 2).reshape(dseq * dbsz, D_PLE)
        xs = _ffn_ple_call(x2.reshape(dseq * dbsz, D_MODEL), ps, *ffn2_args)
        outs[3].append(cst); outs[4].append(lst); outs[5].append(hst)

    y_prompt = xp.reshape(bsz, seq, D_MODEL)
    y_sample = xs.reshape(dseq, dbsz, D_MODEL).transpose(1, 0, 2)
    return (y_prompt, y_sample) + tuple(jnp.stack(o) for o in outs)
```

```python
import functools

import jax
import jax.numpy as jnp
from jax import lax
from jax.experimental import pallas as pl
from jax.experimental.pallas import tpu as pltpu

D_MODEL = 1024
D_CONV = 512
D_LRU = 512
D_IN = 2 * D_CONV + 2 * D_LRU
D_FF = 2816
D_PLE = 256
LRU_HEADS = 8
CONV_WIDTH = 31
LRU_CONV_WIDTH = 4
LRU_C = 8.0
EPS = 1e-6
FFN_RES_WEIGHT = 0.5

SUBLANES = 8
LANES = 128
VMEM_LIMIT_BYTES = 56 * 1024 * 1024

TOKEN_TILE = 512
TIME_TILE = 512
CONV_ROWS = 64
CONV_PAD = 32
LCONV_PAD = 8
SAMPLE_BATCH_TILE = 32
ANCHOR_ROWS = 16

FF_CHUNKS = tuple((lo, min(512, D_FF - lo)) for lo in range(0, D_FF, 512))

FF_COLS = 256
DOWN_COLS = 256
IN_COLS = 512
GATE_COLS = 256
PART_ROWS = 128
A_PER_FF = (2, 2, 2, 2, 1, 1, 1, 1, 1, 1, 1)
A_PER_DOWN = (2, 2, 2, 3)
A_PER_IN = (1, 2, 2, 0)
B_PER_FF = (2, 2, 2, 2, 2, 2, 2, 2, 2, 1, 1)
B_PER_DOWN = (0, 0, 0, 0)
B_PER_GATE = (3, 3, 3, 3)

_bf16 = jnp.bfloat16
_f32 = jnp.float32


def _rms(x, g):
    ms = jnp.mean(x * x, axis=-1, keepdims=True)
    return x * lax.rsqrt(ms + EPS) * g


def _dot(a, b):
    return jnp.dot(a, b, preferred_element_type=_f32)


def _ffn(x, pre_g, post_g, wgu_ref, wd_ref, act_ref):
    h = _rms(x, pre_g).astype(_bf16)
    for lo, n in FF_CHUNKS:
        gate = _dot(h, wgu_ref[:, lo:lo + n])
        up = _dot(h, wgu_ref[:, D_FF + lo:D_FF + lo + n])
        act_ref[:, lo:lo + n] = (gate * jax.nn.sigmoid(gate) * up).astype(_bf16)
    y = _dot(act_ref[...], wd_ref[...])
    return x + FFN_RES_WEIGHT * _rms(y, post_g)


def _ffn_in_kernel(x_ref, pre_g, post_g, wgu, wd, mix_g, w_in, x1_ref, u_ref, act_ref):
    x1 = _ffn(x_ref[...], pre_g[...], post_g[...], wgu, wd, act_ref)
    x1_ref[...] = x1
    u_ref[...] = _dot(_rms(x1, mix_g[...]).astype(_bf16), w_in[...])


def _ffn_ple_kernel(x_ref, p_ref, pre_g, post_g, wgu, wd, ple_g, w_gate, w_proj, ple_post_g,
                    y_ref, act_ref):
    x3 = _ffn(x_ref[...], pre_g[...], post_g[...], wgu, wd, act_ref)
    gate = jax.nn.sigmoid(_dot(_rms(x3, ple_g[...]).astype(_bf16), w_gate[...]))
    e = _dot(p_ref[...].astype(_bf16), w_proj[...])
    y_ref[...] = x3 + _rms(gate * e, ple_post_g[...])


def _const_spec(shape):
    zeros = (0,) * len(shape)
    return pl.BlockSpec(shape, lambda *_: zeros, pipeline_mode=pl.Buffered(1))


def _row_spec(tile, width):
    return pl.BlockSpec((tile, width), lambda i: (i, 0))


def _ffn_in_call(x, pre_g, post_g, wgu, wd, mix_g, w_in):
    n = x.shape[0]
    tile = min(TOKEN_TILE, n)
    return pl.pallas_call(
        _ffn_in_kernel,
        grid=(n // tile,),
        in_specs=[_row_spec(tile, D_MODEL), _const_spec(pre_g.shape), _const_spec(post_g.shape),
                  _const_spec(wgu.shape), _const_spec(wd.shape), _const_spec(mix_g.shape),
                  _const_spec(w_in.shape)],
        out_specs=[_row_spec(tile, D_MODEL), _row_spec(tile, D_IN)],
        out_shape=[jax.ShapeDtypeStruct((n, D_MODEL), _f32), jax.ShapeDtypeStruct((n, D_IN), _f32)],
        scratch_shapes=[pltpu.VMEM((tile, D_FF), _bf16)],
        compiler_params=pltpu.CompilerParams(dimension_semantics=("arbitrary",),
                                             vmem_limit_bytes=VMEM_LIMIT_BYTES),
        name="ffn1_in_proj",
    )(x, pre_g, post_g, wgu, wd, mix_g, w_in)


def _ffn_ple_call(x, p, pre_g, post_g, wgu, wd, ple_g, w_gate, w_proj, ple_post_g):
    n = x.shape[0]
    tile = min(TOKEN_TILE, n)
    return pl.pallas_call(
        _ffn_ple_kernel,
        grid=(n // tile,),
        in_specs=[_row_spec(tile, D_MODEL), _row_spec(tile, D_PLE),
                  _const_spec(pre_g.shape), _const_spec(post_g.shape),
                  _const_spec(wgu.shape), _const_spec(wd.shape), _const_spec(ple_g.shape),
                  _const_spec(w_gate.shape), _const_spec(w_proj.shape), _const_spec(ple_post_g.shape)],
        out_specs=_row_spec(tile, D_MODEL),
        out_shape=jax.ShapeDtypeStruct((n, D_MODEL), _f32),
        scratch_shapes=[pltpu.VMEM((tile, D_FF), _bf16)],
        compiler_params=pltpu.CompilerParams(dimension_semantics=("arbitrary",),
                                             vmem_limit_bytes=VMEM_LIMIT_BYTES),
        name="ffn2_ple",
    )(x, p, pre_g, post_g, wgu, wd, ple_g, w_gate, w_proj, ple_post_g)


def _lru_gates(xc, w_gates, ba, bx):
    gates = _dot(xc.astype(_bf16), w_gates)
    return jax.nn.sigmoid(gates[:, :D_LRU] + ba), jax.nn.sigmoid(gates[:, D_LRU:] + bx)


def _lru_ab(xc, r, i_g, lam):
    neg_lam = -lam
    softplus = jnp.maximum(neg_lam, 0.0) + jnp.log1p(jnp.exp(-jnp.abs(neg_lam)))
    log_a = -LRU_C * r * softplus
    a = jnp.exp(log_a)
    b = jnp.sqrt(-jnp.tanh(log_a) * (a * a + 1.0)) * (i_g * xc)
    return a, b


def _causal_taps(buf, r0, c0, rows, pad, width, w_ref, b_ref):
    first = pad - (width - 1)
    x = buf[r0:r0 + rows + pad, c0:c0 + LANES]
    acc = jnp.broadcast_to(b_ref[:, c0:c0 + LANES], (rows, LANES))
    for phase in range(SUBLANES):
        taps = [(q, q * SUBLANES + phase - first) for q in range(pad // SUBLANES + 1)]
        taps = [(q, k) for q, k in taps if 0 <= k < width]
        if not taps:
            continue
        xs = x if phase == 0 else pltpu.roll(x, rows + pad - phase, axis=0)
        for q, k in taps:
            acc = acc + w_ref[k:k + 1, c0:c0 + LANES] * xs[q * SUBLANES:q * SUBLANES + rows, :]
    return acc


def _anchored(value, side, count):
    pieces = [p for _ in range(count) for p in next(side)]
    if not pieces:
        return value
    folded = []
    for p in pieces:
        m = jnp.min(p.reshape(-1, SUBLANES, p.shape[-1]), axis=0)
        folded += [m[:, c0:c0 + LANES] for c0 in range(0, m.shape[1], LANES)]
    zero = jnp.minimum(jnp.abs(functools.reduce(jnp.minimum, folded)), 0.0)
    zero = jnp.concatenate([zero] * (ANCHOR_ROWS // SUBLANES), axis=0)
    zero = jnp.concatenate([zero] * (value.shape[1] // LANES), axis=1)
    split = value.shape[0] - ANCHOR_ROWS
    return jnp.concatenate([value[0:split], value[split:] + zero], axis=0)


def _ffn_anchored(x, pre_g, post_g, wgu_ref, wd_ref, act_ref, side, per_ff, per_down):
    h = _rms(x, pre_g).astype(_bf16)
    for lo, count in zip(range(0, D_FF, FF_COLS), per_ff, strict=True):
        gate = _dot(h, wgu_ref[:, lo:lo + FF_COLS])
        up = _dot(h, wgu_ref[:, D_FF + lo:D_FF + lo + FF_COLS])
        act_ref[:, lo:lo + FF_COLS] = _anchored(gate * jax.nn.sigmoid(gate) * up, side, count).astype(_bf16)
    ys = [_anchored(_dot(act_ref[...], wd_ref[:, lo:lo + DOWN_COLS]), side, count)
          for lo, count in zip(range(0, D_MODEL, DOWN_COLS), per_down, strict=True)]
    return x + FFN_RES_WEIGHT * _rms(jnp.concatenate(ys, axis=1), post_g)


def _lru_steps(fresh, lconv_w, lconv_b, w_gates, ba, bx, lam, ubuf, g_ref, yb_ref, lconv_st_ref, h_st_ref,
               xbuf, xcbuf, hbuf, hcar):
    tt = ubuf.shape[0]
    parts = [(lo, lo + PART_ROWS) for lo in range(0, tt, PART_ROWS)]

    for lo, hi in parts:
        g = ubuf[lo:hi, 0:D_CONV] * jax.nn.sigmoid(ubuf[lo:hi, D_CONV:2 * D_CONV])
        g_ref[lo:hi, :] = g
        yield [g]

    for lo, hi in parts:
        blocks = []
        for r0 in range(lo, hi, CONV_ROWS):
            for c0 in range(0, D_LRU, LANES):
                xc = _causal_taps(xbuf, r0, c0, CONV_ROWS, LCONV_PAD, LRU_CONV_WIDTH, lconv_w, lconv_b)
                xcbuf[r0:r0 + CONV_ROWS, c0:c0 + LANES] = xc
                blocks.append(xc)
        yield blocks
    lconv_st_ref[0] = xbuf[pl.ds(LCONV_PAD + tt - (LRU_CONV_WIDTH - 1), LRU_CONV_WIDTH - 1), :]

    gates = []
    for lo, hi in parts:
        xc = xcbuf[lo:hi, :]
        r, i_g = _lru_gates(xc, w_gates[...], ba[...], bx[...])
        gates.append((xc, r, i_g))
        yield [r, i_g]
    coeffs = []
    for xc, r, i_g in gates:
        a, b = _lru_ab(xc, r, i_g, lam[...])
        coeffs.append((a, b))
        yield [b]

    scanned = []
    for a, b in coeffs:
        for lo in range(0, PART_ROWS, CONV_ROWS):
            aq = a[lo:lo + CONV_ROWS].reshape(CONV_ROWS // SUBLANES, SUBLANES, D_LRU)
            bq = b[lo:lo + CONV_ROWS].reshape(aq.shape)
            row = lax.broadcasted_iota(jnp.int32, aq.shape, 1)
            shift = 1
            while shift < SUBLANES:
                keep = row >= shift
                bq = jnp.where(keep, aq * pltpu.roll(bq, shift, axis=1) + bq, bq)
                aq = jnp.where(keep, aq * pltpu.roll(aq, shift, axis=1), aq)
                shift *= 2
            scanned.append((aq, bq))
            yield [aq, bq]

    carry = jnp.where(fresh, 0.0, hcar[...])
    r0 = 0
    for aq, bq in scanned:
        for i in range(aq.shape[0]):
            rows = aq[i] * carry + bq[i]
            hbuf[r0:r0 + SUBLANES, :] = rows
            carry = rows[SUBLANES - 1:SUBLANES, :]
            r0 += SUBLANES
    hcar[...] = carry
    h_st_ref[0] = carry
    yield [rows]

    for lo, hi in parts:
        yb = hbuf[lo:hi, :] * jax.nn.gelu(ubuf[lo:hi, 2 * D_CONV + D_LRU:D_IN])
        yb_ref[lo:hi, :] = yb.astype(_bf16)
        yield [yb]


def _prompt_a_kernel(x_ref, pre_g, post_g, wgu, wd, mix_g, w_in, lconv_w, lconv_b, w_gates, ba, bx, lam,
                     x1_ref, g_ref, yb_ref, lconv_st_ref, h_st_ref,
                     act_ref, ubuf, x1buf, xbuf, xcbuf, hbuf, hcar, *, tiles_per_seq):
    s = pl.program_id(0)
    tt = x_ref.shape[0]

    @pl.when(s == 0)
    def _():
        ubuf[...] = jnp.zeros(ubuf.shape, _f32)
        x1buf[...] = jnp.zeros(x1buf.shape, _f32)
        xbuf[...] = jnp.zeros(xbuf.shape, _f32)
        hcar[...] = jnp.zeros(hcar.shape, _f32)

    fresh = lax.rem(s + tiles_per_seq - 1, tiles_per_seq) == 0
    x1_ref[...] = x1buf[...]
    xbuf[0:LCONV_PAD, :] = jnp.where(fresh, 0.0, xbuf[tt:tt + LCONV_PAD, :])
    xbuf[LCONV_PAD:LCONV_PAD + tt, :] = ubuf[:, 2 * D_CONV:2 * D_CONV + D_LRU]
    side = _lru_steps(fresh, lconv_w, lconv_b, w_gates, ba, bx, lam, ubuf, g_ref, yb_ref, lconv_st_ref, h_st_ref,
                      xbuf, xcbuf, hbuf, hcar)

    x1 = _ffn_anchored(x_ref[...], pre_g[...], post_g[...], wgu, wd, act_ref, side, A_PER_FF, A_PER_DOWN)
    x1buf[...] = x1
    hm = _rms(x1, mix_g[...]).astype(_bf16)
    for lo, count in zip(range(0, D_IN, IN_COLS), A_PER_IN, strict=True):
        ubuf[:, lo:lo + IN_COLS] = _anchored(_dot(hm, w_in[:, lo:lo + IN_COLS]), side, count)
    assert next(side, None) is None


def _prompt_a_call(x, bsz, pre_g, post_g, wgu, wd, mix_g, w_in, lconv_w, lconv_b, w_gates, ba, bx, lam):
    n = x.shape[0]
    tt = TIME_TILE
    tiles = n // tt
    tiles_per_seq = tiles // bsz
    consts = (pre_g, post_g, wgu, wd, mix_g, w_in, lconv_w, lconv_b, w_gates, ba, bx, lam)
    prev_tile = lambda s: (jnp.maximum(s - 1, 0), 0)
    prev_seq = lambda s: (jnp.maximum(s - 1, 0) // tiles_per_seq, 0, 0)
    return pl.pallas_call(
        functools.partial(_prompt_a_kernel, tiles_per_seq=tiles_per_seq),
        grid=(tiles + 1,),
        in_specs=[pl.BlockSpec((tt, D_MODEL), lambda s: (jnp.minimum(s, tiles - 1), 0))]
                 + [_const_spec(c.shape) for c in consts],
        out_specs=[pl.BlockSpec((tt, D_MODEL), prev_tile),
                   pl.BlockSpec((tt, D_CONV), prev_tile),
                   pl.BlockSpec((tt, D_LRU), prev_tile),
                   pl.BlockSpec((1, LRU_CONV_WIDTH - 1, D_LRU), prev_seq),
                   pl.BlockSpec((1, 1, D_LRU), prev_seq)],
        out_shape=[jax.ShapeDtypeStruct((n, D_MODEL), _f32),
                   jax.ShapeDtypeStruct((n, D_CONV), _f32),
                   jax.ShapeDtypeStruct((n, D_LRU), _bf16),
                   jax.ShapeDtypeStruct((bsz, LRU_CONV_WIDTH - 1, D_LRU), _f32),
                   jax.ShapeDtypeStruct((bsz, 1, D_LRU), _f32)],
        scratch_shapes=[pltpu.VMEM((tt, D_FF), _bf16),
                        pltpu.VMEM((tt, D_IN), _f32),
                        pltpu.VMEM((tt, D_MODEL), _f32),
                        pltpu.VMEM((LCONV_PAD + tt, D_LRU), _f32),
                        pltpu.VMEM((tt, D_LRU), _f32),
                        pltpu.VMEM((tt, D_LRU), _f32),
                        pltpu.VMEM((1, D_LRU), _f32)],
        compiler_params=pltpu.CompilerParams(dimension_semantics=("arbitrary",),
                                             vmem_limit_bytes=VMEM_LIMIT_BYTES),
        name="prompt_ffn1_lru",
    )(x, *consts)


def _conv_steps(conv_w, conv_b, conv_g, gbuf, cbuf):
    tt = cbuf.shape[0]
    for r0 in range(0, tt, CONV_ROWS):
        cols = []
        for c0 in range(0, D_CONV, LANES):
            cols.append(_causal_taps(gbuf, r0, c0, CONV_ROWS, CONV_PAD, CONV_WIDTH, conv_w, conv_b))
            if len(cols) * LANES < D_CONV:
                yield [cols[-1]]
        cn = _rms(jnp.concatenate(cols, axis=1), conv_g[...])
        c = cn * jax.nn.sigmoid(cn)
        cbuf[r0:r0 + CONV_ROWS, :] = c.astype(_bf16)
        yield [c]


def _prompt_b_kernel(g_ref, yb_ref, x1_ref, p_ref, conv_w, conv_b, conv_g, w_out, mix_post_g,
                     pre_g, post_g, wgu, wd, ple_g, w_gate, w_proj, ple_post_g,
                     y_ref, conv_st_ref, act_ref, gbuf, cbuf, *, tiles_per_seq):
    s = pl.program_id(0)
    tt = g_ref.shape[0]

    @pl.when(s == 0)
    def _():
        gbuf[...] = jnp.zeros(gbuf.shape, _f32)
        cbuf[...] = jnp.zeros(cbuf.shape, _bf16)

    out = _dot(cbuf[...], w_out[0:D_CONV, :]) + _dot(yb_ref[...], w_out[D_CONV:D_CONV + D_LRU, :])
    x2 = x1_ref[...] + _rms(out, mix_post_g[...])

    fresh = lax.rem(s, tiles_per_seq) == 0
    gbuf[0:CONV_PAD, :] = jnp.where(fresh, 0.0, gbuf[tt:tt + CONV_PAD, :])
    gbuf[CONV_PAD:CONV_PAD + tt, :] = g_ref[...]
    conv_st_ref[0] = gbuf[pl.ds(CONV_PAD + tt - (CONV_WIDTH - 1), CONV_WIDTH - 1), :]
    side = _conv_steps(conv_w, conv_b, conv_g, gbuf, cbuf)

    x3 = _ffn_anchored(x2, pre_g[...], post_g[...], wgu, wd, act_ref, side, B_PER_FF, B_PER_DOWN)
    hg = _rms(x3, ple_g[...]).astype(_bf16)
    gate = jnp.concatenate(
        [_anchored(_dot(hg, w_gate[:, lo:lo + GATE_COLS]), side, count)
         for lo, count in zip(range(0, D_MODEL, GATE_COLS), B_PER_GATE, strict=True)], axis=1)
    assert next(side, None) is None
    e = _dot(p_ref[...].astype(_bf16), w_proj[...])
    y_ref[...] = x3 + _rms(jax.nn.sigmoid(gate) * e, ple_post_g[...])


def _prompt_b_call(g, yb, x1, p, bsz, conv_w, conv_b, conv_g, w_out, mix_post_g, pre_g, post_g, wgu, wd, ple_g,
                   w_gate, w_proj, ple_post_g):
    n = x1.shape[0]
    tt = TIME_TILE
    tiles = n // tt
    tiles_per_seq = tiles // bsz
    consts = (conv_w, conv_b, conv_g, w_out, mix_post_g, pre_g, post_g, wgu, wd, ple_g, w_gate, w_proj,
              ple_post_g)
    this_tile = lambda s: (jnp.minimum(s, tiles - 1), 0)
    prev_tile = lambda s: (jnp.maximum(s - 1, 0), 0)
    return pl.pallas_call(
        functools.partial(_prompt_b_kernel, tiles_per_seq=tiles_per_seq),
        grid=(tiles + 1,),
        in_specs=[pl.BlockSpec((tt, D_CONV), this_tile),
                  pl.BlockSpec((tt, D_LRU), prev_tile),
                  pl.BlockSpec((tt, D_MODEL), prev_tile),
                  pl.BlockSpec((tt, D_PLE), prev_tile)]
                 + [_const_spec(c.shape) for c in consts],
        out_specs=[pl.BlockSpec((tt, D_MODEL), prev_tile),
                   pl.BlockSpec((1, CONV_WIDTH - 1, D_CONV),
                                lambda s: (jnp.minimum(s, tiles - 1) // tiles_per_seq, 0, 0))],
        out_shape=[jax.ShapeDtypeStruct((n, D_MODEL), _f32),
                   jax.ShapeDtypeStruct((bsz, CONV_WIDTH - 1, D_CONV), _f32)],
        scratch_shapes=[pltpu.VMEM((tt, D_FF), _bf16),
                        pltpu.VMEM((CONV_PAD + tt, D_CONV), _f32),
                        pltpu.VMEM((tt, D_CONV), _bf16)],
        compiler_params=pltpu.CompilerParams(dimension_semantics=("arbitrary",),
                                             vmem_limit_bytes=VMEM_LIMIT_BYTES),
        name="prompt_conv_ffn2_ple",
    )(g, yb, x1, p, *consts)


def _mix_sample_kernel(u_ref, x1_ref, cache_ref, lstate_ref, h0_ref,
                       conv_w, conv_b, conv_g, lconv_w, lconv_b, w_gates, ba, bx, lam, w_out, post_g,
                       x2_ref, conv_st_ref, lconv_st_ref, h_st_ref):
    steps = u_ref.shape[0]
    past = CONV_WIDTH - 1
    lpast = LRU_CONV_WIDTH - 1

    gp = [cache_ref[:, j, :] for j in range(past)]
    gp += [u_ref[t, :, 0:D_CONV] * jax.nn.sigmoid(u_ref[t, :, D_CONV:2 * D_CONV]) for t in range(steps)]
    for j in range(past):
        conv_st_ref[:, j, :] = gp[j + steps]
    c_rows = []
    for t in range(steps):
        acc = jnp.broadcast_to(conv_b[...], gp[0].shape)
        for k in range(CONV_WIDTH):
            acc = acc + conv_w[k:k + 1, :] * gp[t + k]
        cn = _rms(acc, conv_g[...])
        c_rows.append(cn * jax.nn.sigmoid(cn))
    c = jnp.concatenate(c_rows, axis=0).astype(_bf16)

    xp = [lstate_ref[:, j, :] for j in range(lpast)]
    xp += [u_ref[t, :, 2 * D_CONV:2 * D_CONV + D_LRU] for t in range(steps)]
    for j in range(lpast):
        lconv_st_ref[:, j, :] = xp[j + steps]
    xc_rows = []
    for t in range(steps):
        acc = jnp.broadcast_to(lconv_b[...], xp[0].shape)
        for k in range(LRU_CONV_WIDTH):
            acc = acc + lconv_w[k:k + 1, :] * xp[t + k]
        xc_rows.append(acc)
    xc = jnp.concatenate(xc_rows, axis=0)
    r, i_g = _lru_gates(xc, w_gates[...], ba[...], bx[...])
    a, b = _lru_ab(xc, r, i_g, lam[...])
    bt = h0_ref.shape[0]
    h = h0_ref[...]
    hs = []
    for t in range(steps):
        h = a[t * bt:(t + 1) * bt, :] * h + b[t * bt:(t + 1) * bt, :]
        hs.append(h)
    h_st_ref[...] = h
    u_gelu = jnp.concatenate([u_ref[t, :, 2 * D_CONV + D_LRU:D_IN] for t in range(steps)], axis=0)
    yb = (jnp.concatenate(hs, axis=0) * jax.nn.gelu(u_gelu)).astype(_bf16)

    out = _dot(c, w_out[0:D_CONV, :]) + _dot(yb, w_out[D_CONV:D_CONV + D_LRU, :])
    x2 = _rms(out, post_g[...])
    for t in range(steps):
        x2_ref[t] = x1_ref[t] + x2[t * bt:(t + 1) * bt, :]


def _mix_sample_call(u, x1, cache, lstate, h0, conv_w, conv_b, conv_g, lconv_w, lconv_b, w_gates, ba, bx,
                     lam, w_out, post_g):
    steps, bsz, _ = u.shape
    bt = SAMPLE_BATCH_TILE
    consts = (conv_w, conv_b, conv_g, lconv_w, lconv_b, w_gates, ba, bx, lam, w_out, post_g)
    return pl.pallas_call(
        _mix_sample_kernel,
        grid=(bsz // bt,),
        in_specs=[pl.BlockSpec((steps, bt, D_IN), lambda i: (0, i, 0)),
                  pl.BlockSpec((steps, bt, D_MODEL), lambda i: (0, i, 0)),
                  pl.BlockSpec((bt, CONV_WIDTH - 1, D_CONV), lambda i: (i, 0, 0)),
                  pl.BlockSpec((bt, LRU_CONV_WIDTH - 1, D_LRU), lambda i: (i, 0, 0)),
                  pl.BlockSpec((bt, D_LRU), lambda i: (i, 0))]
                 + [_const_spec(c.shape) for c in consts],
        out_specs=[pl.BlockSpec((steps, bt, D_MODEL), lambda i: (0, i, 0)),
                   pl.BlockSpec((bt, CONV_WIDTH - 1, D_CONV), lambda i: (i, 0, 0)),
                   pl.BlockSpec((bt, LRU_CONV_WIDTH - 1, D_LRU), lambda i: (i, 0, 0)),
                   pl.BlockSpec((bt, D_LRU), lambda i: (i, 0))],
        out_shape=[jax.ShapeDtypeStruct((steps, bsz, D_MODEL), _f32),
                   jax.ShapeDtypeStruct((bsz, CONV_WIDTH - 1, D_CONV), _f32),
                   jax.ShapeDtypeStruct((bsz, LRU_CONV_WIDTH - 1, D_LRU), _f32),
                   jax.ShapeDtypeStruct((bsz, D_LRU), _f32)],
        compiler_params=pltpu.CompilerParams(dimension_semantics=("arbitrary",),
                                             vmem_limit_bytes=VMEM_LIMIT_BYTES),
        name="mix_sample",
    )(u, x1, cache, lstate, h0, *consts)


def _block_diag(w):
    heads, d, _ = w.shape
    eye = jnp.eye(heads, dtype=w.dtype)
    return (eye[:, None, :, None] * w[:, :, None, :]).reshape(heads * d, heads * d)


def kernel(x_prompt, x_sample, cache_conv, state_lru_conv, state_lru_h, p_prompt, p_sample, ffn1_pre_g, ffn1_post_g, ffn1_w_gu, ffn1_w_down, mix_pre_g, mix_post_g, w_in, conv_w, conv_b, conv_norm_g, lru_conv_w, lru_conv_b, lru_wa, lru_ba, lru_wx, lru_bx, lru_lambda, w_out, ffn2_pre_g, ffn2_post_g, ffn2_w_gu, ffn2_w_down, ple_norm_g, ple_w_gate, ple_w_proj, ple_post_g):
    depth = ffn1_w_gu.shape[0]
    bsz, seq, _ = x_prompt.shape
    dbsz, dseq, _ = x_sample.shape

    xp = x_prompt.reshape(bsz * seq, D_MODEL)
    xs = x_sample.transpose(1, 0, 2).reshape(dseq * dbsz, D_MODEL)
    outs = [[] for _ in range(6)]
    for l in range(depth):
        row = lambda v: v[l:l + 1]
        wgu1, wd1 = ffn1_w_gu[l].astype(_bf16), ffn1_w_down[l].astype(_bf16)
        wgu2, wd2 = ffn2_w_gu[l].astype(_bf16), ffn2_w_down[l].astype(_bf16)
        w_in_l, w_out_l = w_in[l].astype(_bf16), w_out[l].astype(_bf16)
        w_gate_l, w_proj_l = ple_w_gate[l].astype(_bf16), ple_w_proj[l].astype(_bf16)
        w_gates = jnp.concatenate([_block_diag(lru_wa[l]), _block_diag(lru_wx[l])], axis=1).astype(_bf16)
        conv_consts = (conv_w[l], row(conv_b), row(conv_norm_g))
        lru_consts = (lru_conv_w[l], row(lru_conv_b), w_gates, row(lru_ba), row(lru_bx), row(lru_lambda))
        ffn1_args = (row(ffn1_pre_g), row(ffn1_post_g), wgu1, wd1, row(mix_pre_g), w_in_l)
        ffn2_args = (row(ffn2_pre_g), row(ffn2_post_g), wgu2, wd2, row(ple_norm_g), w_gate_l, w_proj_l,
                     row(ple_post_g))

        x1, g, yb, lst, hst = _prompt_a_call(xp, bsz, *ffn1_args, *lru_consts)
        xp, cst = _prompt_b_call(g, yb, x1, p_prompt[l].reshape(bsz * seq, D_PLE), bsz, *conv_consts, w_out_l,
                                 row(mix_post_g), *ffn2_args)
        outs[0].append(cst); outs[1].append(lst); outs[2].append(hst.reshape(bsz, D_LRU))

        x1, u = _ffn_in_call(xs, *ffn1_args)
        x2, cst, lst, hst = _mix_sample_call(u.reshape(dseq, dbsz, D_IN), x1.reshape(dseq, dbsz, D_MODEL),
                                             cache_conv[l], state_lru_conv[l], state_lru_h[l], *conv_consts,
                                             *lru_consts, w_out_l, row(mix_post_g))
        ps = p_sample[l].transpose(1, 0, 2).reshape(dseq * dbsz, D_PLE)
        xs = _ffn_ple_call(x2.reshape(dseq * dbsz, D_MODEL), ps, *ffn2_args)
        outs[3].append(cst); outs[4].append(lst); outs[5].append(hst)

    y_prompt = xp.reshape(bsz, seq, D_MODEL)
    y_sample = xs.reshape(dseq, dbsz, D_MODEL).transpose(1, 0, 2)
    return (y_prompt, y_sample) + tuple(o[0][None] if depth == 1 else jnp.stack(o) for o in outs)
```

```python
import functools

import jax
import jax.numpy as jnp
from jax import lax
from jax.experimental import pallas as pl
from jax.experimental.pallas import tpu as pltpu

D_MODEL = 1024
D_CONV = 512
D_LRU = 512
D_IN = 2 * D_CONV + 2 * D_LRU
D_FF = 2816
D_PLE = 256
LRU_HEADS = 8
CONV_WIDTH = 31
LRU_CONV_WIDTH = 4
LRU_C = 8.0
EPS = 1e-6
FFN_RES_WEIGHT = 0.5

SUBLANES = 8
LANES = 128
VMEM_LIMIT_BYTES = 56 * 1024 * 1024

TOKEN_TILE = 512
TIME_TILE = 512
CONV_ROWS = 64
CONV_PAD = 32
LCONV_PAD = 8
SAMPLE_BATCH_TILE = 32
ANCHOR_ROWS = 16

FF_CHUNKS = tuple((lo, min(512, D_FF - lo)) for lo in range(0, D_FF, 512))

FF_COLS = 256
DOWN_COLS = 256
IN_COLS = 512
PART_ROWS = 128
PIECES_PER_FF = ("CCCC", "LC", "LC", "LC", "LC", "LC", "LC", "LC", "LC", "LC", "LC")
PIECES_PER_DOWN = ("", "LLCCC", "LLCCC", "LLCC")
PIECES_PER_IN = ("LLLLCCCC", "LCCCCCC", "LLLLCCCC", "F")

_bf16 = jnp.bfloat16
_f32 = jnp.float32


def _rms(x, g):
    ms = jnp.mean(x * x, axis=-1, keepdims=True)
    return x * lax.rsqrt(ms + EPS) * g


def _dot(a, b):
    return jnp.dot(a, b, preferred_element_type=_f32)


def _ffn(x, pre_g, post_g, wgu_ref, wd_ref, act_ref):
    h = _rms(x, pre_g).astype(_bf16)
    for lo, n in FF_CHUNKS:
        gate = _dot(h, wgu_ref[:, lo:lo + n])
        up = _dot(h, wgu_ref[:, D_FF + lo:D_FF + lo + n])
        act_ref[:, lo:lo + n] = (gate * jax.nn.sigmoid(gate) * up).astype(_bf16)
    y = _dot(act_ref[...], wd_ref[...])
    return x + FFN_RES_WEIGHT * _rms(y, post_g)


def _ffn_in_kernel(x_ref, pre_g, post_g, wgu, wd, mix_g, w_in, x1_ref, u_ref, act_ref):
    x1 = _ffn(x_ref[...], pre_g[...], post_g[...], wgu, wd, act_ref)
    x1_ref[...] = x1
    u_ref[...] = _dot(_rms(x1, mix_g[...]).astype(_bf16), w_in[...])


def _ffn_ple_kernel(x_ref, p_ref, pre_g, post_g, wgu, wd, ple_g, w_gate, w_proj, ple_post_g,
                    y_ref, act_ref):
    x3 = _ffn(x_ref[...], pre_g[...], post_g[...], wgu, wd, act_ref)
    gate = jax.nn.sigmoid(_dot(_rms(x3, ple_g[...]).astype(_bf16), w_gate[...]))
    e = _dot(p_ref[...].astype(_bf16), w_proj[...])
    y_ref[...] = x3 + _rms(gate * e, ple_post_g[...])


def _const_spec(shape):
    zeros = (0,) * len(shape)
    return pl.BlockSpec(shape, lambda *_: zeros, pipeline_mode=pl.Buffered(1))


def _row_spec(tile, width):
    return pl.BlockSpec((tile, width), lambda i: (i, 0))


def _ffn_in_call(x, pre_g, post_g, wgu, wd, mix_g, w_in):
    n = x.shape[0]
    tile = min(TOKEN_TILE, n)
    return pl.pallas_call(
        _ffn_in_kernel,
        grid=(n // tile,),
        in_specs=[_row_spec(tile, D_MODEL), _const_spec(pre_g.shape), _const_spec(post_g.shape),
                  _const_spec(wgu.shape), _const_spec(wd.shape), _const_spec(mix_g.shape),
                  _const_spec(w_in.shape)],
        out_specs=[_row_spec(tile, D_MODEL), _row_spec(tile, D_IN)],
        out_shape=[jax.ShapeDtypeStruct((n, D_MODEL), _f32), jax.ShapeDtypeStruct((n, D_IN), _f32)],
        scratch_shapes=[pltpu.VMEM((tile, D_FF), _bf16)],
        compiler_params=pltpu.CompilerParams(dimension_semantics=("arbitrary",),
                                             vmem_limit_bytes=VMEM_LIMIT_BYTES),
        name="ffn1_in_proj",
    )(x, pre_g, post_g, wgu, wd, mix_g, w_in)


def _ffn_ple_call(x, p, pre_g, post_g, wgu, wd, ple_g, w_gate, w_proj, ple_post_g):
    n = x.shape[0]
    tile = min(TOKEN_TILE, n)
    return pl.pallas_call(
        _ffn_ple_kernel,
        grid=(n // tile,),
        in_specs=[_row_spec(tile, D_MODEL), _row_spec(tile, D_PLE),
                  _const_spec(pre_g.shape), _const_spec(post_g.shape),
                  _const_spec(wgu.shape), _const_spec(wd.shape), _const_spec(ple_g.shape),
                  _const_spec(w_gate.shape), _const_spec(w_proj.shape), _const_spec(ple_post_g.shape)],
        out_specs=_row_spec(tile, D_MODEL),
        out_shape=jax.ShapeDtypeStruct((n, D_MODEL), _f32),
        scratch_shapes=[pltpu.VMEM((tile, D_FF), _bf16)],
        compiler_params=pltpu.CompilerParams(dimension_semantics=("arbitrary",),
                                             vmem_limit_bytes=VMEM_LIMIT_BYTES),
        name="ffn2_ple",
    )(x, p, pre_g, post_g, wgu, wd, ple_g, w_gate, w_proj, ple_post_g)


def _lru_gates(xc, w_gates, ba, bx):
    gates = _dot(xc.astype(_bf16), w_gates)
    return jax.nn.sigmoid(gates[:, :D_LRU] + ba), jax.nn.sigmoid(gates[:, D_LRU:] + bx)


def _lru_ab(xc, r, i_g, lam):
    neg_lam = -lam
    softplus = jnp.maximum(neg_lam, 0.0) + jnp.log1p(jnp.exp(-jnp.abs(neg_lam)))
    log_a = -LRU_C * r * softplus
    a = jnp.exp(log_a)
    b = jnp.sqrt(-jnp.tanh(log_a) * (a * a + 1.0)) * (i_g * xc)
    return a, b


def _causal_taps(buf, r0, c0, rows, pad, width, w_ref, b_ref):
    first = pad - (width - 1)
    x = buf[r0:r0 + rows + pad, c0:c0 + LANES]
    acc = jnp.broadcast_to(b_ref[:, c0:c0 + LANES], (rows, LANES))
    for phase in range(SUBLANES):
        taps = [(q, q * SUBLANES + phase - first) for q in range(pad // SUBLANES + 1)]
        taps = [(q, k) for q, k in taps if 0 <= k < width]
        if not taps:
            continue
        xs = x if phase == 0 else pltpu.roll(x, rows + pad - phase, axis=0)
        for q, k in taps:
            acc = acc + w_ref[k:k + 1, c0:c0 + LANES] * xs[q * SUBLANES:q * SUBLANES + rows, :]
    return acc


def _anchored(value, chains, which):
    pieces = [p for c in which for p in next(chains[c])]
    if not pieces:
        return value
    folded = []
    for p in pieces:
        m = jnp.min(p.reshape(-1, SUBLANES, p.shape[-1]), axis=0)
        folded += [m[:, c0:c0 + LANES] for c0 in range(0, m.shape[1], LANES)]
    zero = jnp.minimum(jnp.abs(functools.reduce(jnp.minimum, folded)), 0.0)
    zero = jnp.concatenate([zero] * (ANCHOR_ROWS // SUBLANES), axis=0).astype(value.dtype)
    zero = jnp.concatenate([zero] * (value.shape[1] // LANES), axis=1)
    split = value.shape[0] - ANCHOR_ROWS
    return jnp.concatenate([value[0:split], value[split:] + zero], axis=0)


def _ffn_anchored(x, pre_g, post_g, wgu_ref, wd_ref, act_ref, chains, per_ff, per_down):
    h = _rms(x, pre_g).astype(_bf16)
    for lo, which in zip(range(0, D_FF, FF_COLS), per_ff, strict=True):
        gate = _dot(h, wgu_ref[:, lo:lo + FF_COLS])
        up = _dot(h, wgu_ref[:, D_FF + lo:D_FF + lo + FF_COLS])
        act_ref[:, lo:lo + FF_COLS] = _anchored(gate * jax.nn.sigmoid(gate) * up, chains, which).astype(_bf16)
    ys = [_dot(_anchored(act_ref[...], chains, which), wd_ref[:, lo:lo + DOWN_COLS])
          for lo, which in zip(range(0, D_MODEL, DOWN_COLS), per_down, strict=True)]
    return x + FFN_RES_WEIGHT * _rms(jnp.concatenate(ys, axis=1), post_g)


def _lru_steps(fresh, lconv_w, lconv_b, w_gates, ba, bx, lam, ubuf, lconv_st_ref, h_st_ref,
               xbuf, xcbuf, hbuf, ybuf, hcar):
    tt = ubuf.shape[0]
    parts = [(lo, lo + PART_ROWS) for lo in range(0, tt, PART_ROWS)]

    for lo, hi in parts:
        blocks = []
        for r0 in range(lo, hi, CONV_ROWS):
            for c0 in range(0, D_LRU, LANES):
                xc = _causal_taps(xbuf, r0, c0, CONV_ROWS, LCONV_PAD, LRU_CONV_WIDTH, lconv_w, lconv_b)
                xcbuf[r0:r0 + CONV_ROWS, c0:c0 + LANES] = xc
                blocks.append(xc)
        yield blocks
    lconv_st_ref[0] = xbuf[pl.ds(LCONV_PAD + tt - (LRU_CONV_WIDTH - 1), LRU_CONV_WIDTH - 1), :]

    gates = []
    for lo, hi in parts:
        xc = xcbuf[lo:hi, :]
        r, i_g = _lru_gates(xc, w_gates[...], ba[...], bx[...])
        gates.append((xc, r, i_g))
        yield [r, i_g]
    coeffs = []
    for xc, r, i_g in gates:
        a, b = _lru_ab(xc, r, i_g, lam[...])
        coeffs.append((a, b))
        yield [b]

    scanned = []
    for a, b in coeffs:
        for lo in range(0, PART_ROWS, CONV_ROWS):
            aq = a[lo:lo + CONV_ROWS].reshape(CONV_ROWS // SUBLANES, SUBLANES, D_LRU)
            bq = b[lo:lo + CONV_ROWS].reshape(aq.shape)
            row = lax.broadcasted_iota(jnp.int32, aq.shape, 1)
            shift = 1
            while shift < SUBLANES:
                keep = row >= shift
                bq = jnp.where(keep, aq * pltpu.roll(bq, shift, axis=1) + bq, bq)
                aq = jnp.where(keep, aq * pltpu.roll(aq, shift, axis=1), aq)
                shift *= 2
            scanned.append((aq, bq))
            yield [aq, bq]

    carry = jnp.where(fresh, 0.0, hcar[...])
    r0 = 0
    for aq, bq in scanned:
        for i in range(aq.shape[0]):
            rows = aq[i] * carry + bq[i]
            hbuf[r0:r0 + SUBLANES, :] = rows
            carry = rows[SUBLANES - 1:SUBLANES, :]
            r0 += SUBLANES
    hcar[...] = carry
    h_st_ref[0] = carry
    yield [rows]

    for lo, hi in parts:
        yb = hbuf[lo:hi, :] * jax.nn.gelu(ubuf[lo:hi, 2 * D_CONV + D_LRU:D_IN])
        ybuf[lo:hi, :] = yb.astype(_bf16)
        yield [yb]


def _conv_steps(conv_w, conv_b, conv_g, ubuf, conv_st_ref, gbuf, cbuf):
    tt = cbuf.shape[0]
    for lo in range(0, tt, PART_ROWS):
        g = ubuf[lo:lo + PART_ROWS, 0:D_CONV] * jax.nn.sigmoid(ubuf[lo:lo + PART_ROWS, D_CONV:2 * D_CONV])
        gbuf[CONV_PAD + lo:CONV_PAD + lo + PART_ROWS, :] = g
        yield [g]
    conv_st_ref[0] = gbuf[pl.ds(CONV_PAD + tt - (CONV_WIDTH - 1), CONV_WIDTH - 1), :]
    for r0 in range(0, tt, CONV_ROWS):
        cols = []
        for c0 in range(0, D_CONV, LANES):
            cols.append(_causal_taps(gbuf, r0, c0, CONV_ROWS, CONV_PAD, CONV_WIDTH, conv_w, conv_b))
            if len(cols) * LANES < D_CONV:
                yield [cols[-1]]
        cn = _rms(jnp.concatenate(cols, axis=1), conv_g[...])
        c = cn * jax.nn.sigmoid(cn)
        cbuf[r0:r0 + CONV_ROWS, :] = c.astype(_bf16)
        yield [c]


def _out_steps(w_out, post_g, x2_ref, cbuf, ybuf):
    out = _dot(cbuf[...], w_out[0:D_CONV, :]) + _dot(ybuf[...], w_out[D_CONV:D_CONV + D_LRU, :])
    delta = _rms(out, post_g[...])
    x2_ref[...] += delta
    yield [delta]


def _prompt_a_kernel(x_ref, pre_g, post_g, wgu, wd, mix_g, w_in,
                     conv_w, conv_b, conv_g, lconv_w, lconv_b, w_gates, ba, bx, lam, w_out, mix_post_g,
                     x2_ref, conv_st_ref, lconv_st_ref, h_st_ref,
                     act_ref, ubuf, x1buf, gbuf, xbuf, cbuf, xcbuf, hbuf, ybuf, hcar, *, tiles_per_seq):
    s = pl.program_id(0)
    tt = x_ref.shape[0]

    @pl.when(s == 0)
    def _():
        ubuf[...] = jnp.zeros(ubuf.shape, _f32)
        x1buf[...] = jnp.zeros(x1buf.shape, _f32)
        gbuf[...] = jnp.zeros(gbuf.shape, _f32)
        xbuf[...] = jnp.zeros(xbuf.shape, _f32)
        hcar[...] = jnp.zeros(hcar.shape, _f32)

    fresh = lax.rem(s + tiles_per_seq - 1, tiles_per_seq) == 0
    x2_ref[...] = x1buf[...]
    gbuf[0:CONV_PAD, :] = jnp.where(fresh, 0.0, gbuf[tt:tt + CONV_PAD, :])
    xbuf[0:LCONV_PAD, :] = jnp.where(fresh, 0.0, xbuf[tt:tt + LCONV_PAD, :])
    xbuf[LCONV_PAD:LCONV_PAD + tt, :] = ubuf[:, 2 * D_CONV:2 * D_CONV + D_LRU]
    chains = {
        "L": _lru_steps(fresh, lconv_w, lconv_b, w_gates, ba, bx, lam, ubuf, lconv_st_ref, h_st_ref,
                        xbuf, xcbuf, hbuf, ybuf, hcar),
        "C": _conv_steps(conv_w, conv_b, conv_g, ubuf, conv_st_ref, gbuf, cbuf),
        "F": _out_steps(w_out, mix_post_g, x2_ref, cbuf, ybuf),
    }

    x1 = _ffn_anchored(x_ref[...], pre_g[...], post_g[...], wgu, wd, act_ref, chains, PIECES_PER_FF,
                       PIECES_PER_DOWN)
    x1buf[...] = x1
    hm = _rms(x1, mix_g[...]).astype(_bf16)
    for lo, which in zip(range(0, D_IN, IN_COLS), PIECES_PER_IN, strict=True):
        ubuf[:, lo:lo + IN_COLS] = _anchored(_dot(hm, w_in[:, lo:lo + IN_COLS]), chains, which)
    assert all(next(chain, None) is None for chain in chains.values())


def _prompt_a_call(x, bsz, pre_g, post_g, wgu, wd, mix_g, w_in, conv_w, conv_b, conv_g, lconv_w, lconv_b,
                   w_gates, ba, bx, lam, w_out, mix_post_g):
    n = x.shape[0]
    tt = TIME_TILE
    tiles = n // tt
    tiles_per_seq = tiles // bsz
    consts = (pre_g, post_g, wgu, wd, mix_g, w_in, conv_w, conv_b, conv_g, lconv_w, lconv_b, w_gates, ba, bx,
              lam, w_out, mix_post_g)
    prev_seq = lambda s: (jnp.maximum(s - 1, 0) // tiles_per_seq, 0, 0)
    return pl.pallas_call(
        functools.partial(_prompt_a_kernel, tiles_per_seq=tiles_per_seq),
        grid=(tiles + 1,),
        in_specs=[pl.BlockSpec((tt, D_MODEL), lambda s: (jnp.minimum(s, tiles - 1), 0))]
                 + [_const_spec(c.shape) for c in consts],
        out_specs=[pl.BlockSpec((tt, D_MODEL), lambda s: (jnp.maximum(s - 1, 0), 0)),
                   pl.BlockSpec((1, CONV_WIDTH - 1, D_CONV), prev_seq),
                   pl.BlockSpec((1, LRU_CONV_WIDTH - 1, D_LRU), prev_seq),
                   pl.BlockSpec((1, 1, D_LRU), prev_seq)],
        out_shape=[jax.ShapeDtypeStruct((n, D_MODEL), _f32),
                   jax.ShapeDtypeStruct((bsz, CONV_WIDTH - 1, D_CONV), _f32),
                   jax.ShapeDtypeStruct((bsz, LRU_CONV_WIDTH - 1, D_LRU), _f32),
                   jax.ShapeDtypeStruct((bsz, 1, D_LRU), _f32)],
        scratch_shapes=[pltpu.VMEM((tt, D_FF), _bf16),
                        pltpu.VMEM((tt, D_IN), _f32),
                        pltpu.VMEM((tt, D_MODEL), _f32),
                        pltpu.VMEM((CONV_PAD + tt, D_CONV), _f32),
                        pltpu.VMEM((LCONV_PAD + tt, D_LRU), _f32),
                        pltpu.VMEM((tt, D_CONV), _bf16),
                        pltpu.VMEM((tt, D_LRU), _f32),
                        pltpu.VMEM((tt, D_LRU), _f32),
                        pltpu.VMEM((tt, D_LRU), _bf16),
                        pltpu.VMEM((1, D_LRU), _f32)],
        compiler_params=pltpu.CompilerParams(dimension_semantics=("arbitrary",),
                                             vmem_limit_bytes=VMEM_LIMIT_BYTES),
        name="prompt_ffn1_mix",
    )(x, *consts)


def _mix_sample_kernel(u_ref, x1_ref, cache_ref, lstate_ref, h0_ref,
                       conv_w, conv_b, conv_g, lconv_w, lconv_b, w_gates, ba, bx, lam, w_out, post_g,
                       x2_ref, conv_st_ref, lconv_st_ref, h_st_ref):
    steps = u_ref.shape[0]
    past = CONV_WIDTH - 1
    lpast = LRU_CONV_WIDTH - 1

    gp = [cache_ref[:, j, :] for j in range(past)]
    gp += [u_ref[t, :, 0:D_CONV] * jax.nn.sigmoid(u_ref[t, :, D_CONV:2 * D_CONV]) for t in range(steps)]
    for j in range(past):
        conv_st_ref[:, j, :] = gp[j + steps]
    c_rows = []
    for t in range(steps):
        acc = jnp.broadcast_to(conv_b[...], gp[0].shape)
        for k in range(CONV_WIDTH):
            acc = acc + conv_w[k:k + 1, :] * gp[t + k]
        cn = _rms(acc, conv_g[...])
        c_rows.append(cn * jax.nn.sigmoid(cn))
    c = jnp.concatenate(c_rows, axis=0).astype(_bf16)

    xp = [lstate_ref[:, j, :] for j in range(lpast)]
    xp += [u_ref[t, :, 2 * D_CONV:2 * D_CONV + D_LRU] for t in range(steps)]
    for j in range(lpast):
        lconv_st_ref[:, j, :] = xp[j + steps]
    xc_rows = []
    for t in range(steps):
        acc = jnp.broadcast_to(lconv_b[...], xp[0].shape)
        for k in range(LRU_CONV_WIDTH):
            acc = acc + lconv_w[k:k + 1, :] * xp[t + k]
        xc_rows.append(acc)
    xc = jnp.concatenate(xc_rows, axis=0)
    r, i_g = _lru_gates(xc, w_gates[...], ba[...], bx[...])
    a, b = _lru_ab(xc, r, i_g, lam[...])
    bt = h0_ref.shape[0]
    h = h0_ref[...]
    hs = []
    for t in range(steps):
        h = a[t * bt:(t + 1) * bt, :] * h + b[t * bt:(t + 1) * bt, :]
        hs.append(h)
    h_st_ref[...] = h
    u_gelu = jnp.concatenate([u_ref[t, :, 2 * D_CONV + D_LRU:D_IN] for t in range(steps)], axis=0)
    yb = (jnp.concatenate(hs, axis=0) * jax.nn.gelu(u_gelu)).astype(_bf16)

    out = _dot(c, w_out[0:D_CONV, :]) + _dot(yb, w_out[D_CONV:D_CONV + D_LRU, :])
    x2 = _rms(out, post_g[...])
    for t in range(steps):
        x2_ref[t] = x1_ref[t] + x2[t * bt:(t + 1) * bt, :]


def _mix_sample_call(u, x1, cache, lstate, h0, conv_w, conv_b, conv_g, lconv_w, lconv_b, w_gates, ba, bx,
                     lam, w_out, post_g):
    steps, bsz, _ = u.shape
    bt = SAMPLE_BATCH_TILE
    consts = (conv_w, conv_b, conv_g, lconv_w, lconv_b, w_gates, ba, bx, lam, w_out, post_g)
    return pl.pallas_call(
        _mix_sample_kernel,
        grid=(bsz // bt,),
        in_specs=[pl.BlockSpec((steps, bt, D_IN), lambda i: (0, i, 0)),
                  pl.BlockSpec((steps, bt, D_MODEL), lambda i: (0, i, 0)),
                  pl.BlockSpec((bt, CONV_WIDTH - 1, D_CONV), lambda i: (i, 0, 0)),
                  pl.BlockSpec((bt, LRU_CONV_WIDTH - 1, D_LRU), lambda i: (i, 0, 0)),
                  pl.BlockSpec((bt, D_LRU), lambda i: (i, 0))]
                 + [_const_spec(c.shape) for c in consts],
        out_specs=[pl.BlockSpec((steps, bt, D_MODEL), lambda i: (0, i, 0)),
                   pl.BlockSpec((bt, CONV_WIDTH - 1, D_CONV), lambda i: (i, 0, 0)),
                   pl.BlockSpec((bt, LRU_CONV_WIDTH - 1, D_LRU), lambda i: (i, 0, 0)),
                   pl.BlockSpec((bt, D_LRU), lambda i: (i, 0))],
        out_shape=[jax.ShapeDtypeStruct((steps, bsz, D_MODEL), _f32),
                   jax.ShapeDtypeStruct((bsz, CONV_WIDTH - 1, D_CONV), _f32),
                   jax.ShapeDtypeStruct((bsz, LRU_CONV_WIDTH - 1, D_LRU), _f32),
                   jax.ShapeDtypeStruct((bsz, D_LRU), _f32)],
        compiler_params=pltpu.CompilerParams(dimension_semantics=("arbitrary",),
                                             vmem_limit_bytes=VMEM_LIMIT_BYTES),
        name="mix_sample",
    )(u, x1, cache, lstate, h0, *consts)


def _block_diag(w):
    heads, d, _ = w.shape
    eye = jnp.eye(heads, dtype=w.dtype)
    return (eye[:, None, :, None] * w[:, :, None, :]).reshape(heads * d, heads * d)


def kernel(x_prompt, x_sample, cache_conv, state_lru_conv, state_lru_h, p_prompt, p_sample, ffn1_pre_g, ffn1_post_g, ffn1_w_gu, ffn1_w_down, mix_pre_g, mix_post_g, w_in, conv_w, conv_b, conv_norm_g, lru_conv_w, lru_conv_b, lru_wa, lru_ba, lru_wx, lru_bx, lru_lambda, w_out, ffn2_pre_g, ffn2_post_g, ffn2_w_gu, ffn2_w_down, ple_norm_g, ple_w_gate, ple_w_proj, ple_post_g):
    depth = ffn1_w_gu.shape[0]
    bsz, seq, _ = x_prompt.shape
    dbsz, dseq, _ = x_sample.shape

    xp = x_prompt.reshape(bsz * seq, D_MODEL)
    xs = x_sample.transpose(1, 0, 2).reshape(dseq * dbsz, D_MODEL)
    outs = [[] for _ in range(6)]
    for l in range(depth):
        row = lambda v: v[l:l + 1]
        wgu1, wd1 = ffn1_w_gu[l].astype(_bf16), ffn1_w_down[l].astype(_bf16)
        wgu2, wd2 = ffn2_w_gu[l].astype(_bf16), ffn2_w_down[l].astype(_bf16)
        w_in_l, w_out_l = w_in[l].astype(_bf16), w_out[l].astype(_bf16)
        w_gate_l, w_proj_l = ple_w_gate[l].astype(_bf16), ple_w_proj[l].astype(_bf16)
        w_gates = jnp.concatenate([_block_diag(lru_wa[l]), _block_diag(lru_wx[l])], axis=1).astype(_bf16)
        conv_consts = (conv_w[l], row(conv_b), row(conv_norm_g))
        lru_consts = (lru_conv_w[l], row(lru_conv_b), w_gates, row(lru_ba), row(lru_bx), row(lru_lambda))
        ffn1_args = (row(ffn1_pre_g), row(ffn1_post_g), wgu1, wd1, row(mix_pre_g), w_in_l)
        ffn2_args = (row(ffn2_pre_g), row(ffn2_post_g), wgu2, wd2, row(ple_norm_g), w_gate_l, w_proj_l,
                     row(ple_post_g))

        x2, cst, lst, hst = _prompt_a_call(xp, bsz, *ffn1_args, *conv_consts, *lru_consts, w_out_l,
                                           row(mix_post_g))
        xp = _ffn_ple_call(x2, p_prompt[l].reshape(bsz * seq, D_PLE), *ffn2_args)
        outs[0].append(cst); outs[1].append(lst); outs[2].append(hst.reshape(bsz, D_LRU))

        x1, u = _ffn_in_call(xs, *ffn1_args)
        x2, cst, lst, hst = _mix_sample_call(u.reshape(dseq, dbsz, D_IN), x1.reshape(dseq, dbsz, D_MODEL),
                                             cache_conv[l], state_lru_conv[l], state_lru_h[l], *conv_consts,
                                             *lru_consts, w_out_l, row(mix_post_g))
        ps = p_sample[l].transpose(1, 0, 2).reshape(dseq * dbsz, D_PLE)
        xs = _ffn_ple_call(x2.reshape(dseq * dbsz, D_MODEL), ps, *ffn2_args)
        outs[3].append(cst); outs[4].append(lst); outs[5].append(hst)

    y_prompt = xp.reshape(bsz, seq, D_MODEL)
    y_sample = xs.reshape(dseq, dbsz, D_MODEL).transpose(1, 0, 2)
    return (y_prompt, y_sample) + tuple(o[0][None] if depth == 1 else jnp.stack(o) for o in outs)
```

```python
import functools

import jax
import jax.numpy as jnp
from jax import lax
from jax.experimental import pallas as pl
from jax.experimental.pallas import tpu as pltpu

D_MODEL = 1024
D_CONV = 512
D_LRU = 512
D_IN = 2 * D_CONV + 2 * D_LRU
D_FF = 2816
D_PLE = 256
LRU_HEADS = 8
CONV_WIDTH = 31
LRU_CONV_WIDTH = 4
LRU_C = 8.0
EPS = 1e-6
FFN_RES_WEIGHT = 0.5

SUBLANES = 8
LANES = 128
VMEM_LIMIT_BYTES = 56 * 1024 * 1024

TOKEN_TILE = 512
TIME_TILE = 512
CONV_ROWS = 64
CONV_PAD = 32
LCONV_PAD = 8
SAMPLE_BATCH_TILE = 32
ANCHOR_ROWS = 16

FF_CHUNKS = tuple((lo, min(512, D_FF - lo)) for lo in range(0, D_FF, 512))

FF_COLS = 256
DOWN_COLS = 256
IN_COLS = 512
PART_ROWS = 128
PIECES_PER_FF = ("CCCC", "LC", "LC", "LC", "LC", "LC", "LC", "LC", "LC", "LC", "LC")
PIECES_PER_DOWN = ("", "LLCCC", "LLCCC", "LLCC")
PIECES_PER_IN = ("LLLLCCCC", "LCCCCCC", "LLLLCCCC", "F")

_bf16 = jnp.bfloat16
_f32 = jnp.float32


def _rms(x, g):
    ms = jnp.mean(x * x, axis=-1, keepdims=True)
    return x * lax.rsqrt(ms + EPS) * g


def _dot(a, b):
    return jnp.dot(a, b, preferred_element_type=_f32)


def _ffn(x, pre_g, post_g, wgu_ref, wd_ref, act_ref):
    h = _rms(x, pre_g).astype(_bf16)
    for lo, n in FF_CHUNKS:
        gate = _dot(h, wgu_ref[:, lo:lo + n])
        up = _dot(h, wgu_ref[:, D_FF + lo:D_FF + lo + n])
        act_ref[:, lo:lo + n] = (gate * jax.nn.sigmoid(gate) * up).astype(_bf16)
    y = _dot(act_ref[...], wd_ref[...])
    return x + FFN_RES_WEIGHT * _rms(y, post_g)


def _ffn_in_kernel(x_ref, pre_g, post_g, wgu, wd, mix_g, w_in, x1_ref, u_ref, act_ref):
    x1 = _ffn(x_ref[...], pre_g[...], post_g[...], wgu, wd, act_ref)
    x1_ref[...] = x1
    u_ref[...] = _dot(_rms(x1, mix_g[...]).astype(_bf16), w_in[...])


def _ffn_ple_kernel(x_ref, p_ref, pre_g, post_g, wgu, wd, ple_g, w_gate, w_proj, ple_post_g,
                    y_ref, act_ref):
    x3 = _ffn(x_ref[...], pre_g[...], post_g[...], wgu, wd, act_ref)
    gate = jax.nn.sigmoid(_dot(_rms(x3, ple_g[...]).astype(_bf16), w_gate[...]))
    e = _dot(p_ref[...].astype(_bf16), w_proj[...])
    y_ref[...] = x3 + _rms(gate * e, ple_post_g[...])


def _const_spec(shape):
    zeros = (0,) * len(shape)
    return pl.BlockSpec(shape, lambda *_: zeros, pipeline_mode=pl.Buffered(1))


def _row_spec(tile, width):
    return pl.BlockSpec((tile, width), lambda i: (i, 0))


def _ffn_in_call(x, pre_g, post_g, wgu, wd, mix_g, w_in):
    n = x.shape[0]
    tile = min(TOKEN_TILE, n)
    return pl.pallas_call(
        _ffn_in_kernel,
        grid=(n // tile,),
        in_specs=[_row_spec(tile, D_MODEL), _const_spec(pre_g.shape), _const_spec(post_g.shape),
                  _const_spec(wgu.shape), _const_spec(wd.shape), _const_spec(mix_g.shape),
                  _const_spec(w_in.shape)],
        out_specs=[_row_spec(tile, D_MODEL), _row_spec(tile, D_IN)],
        out_shape=[jax.ShapeDtypeStruct((n, D_MODEL), _f32), jax.ShapeDtypeStruct((n, D_IN), _f32)],
        scratch_shapes=[pltpu.VMEM((tile, D_FF), _bf16)],
        compiler_params=pltpu.CompilerParams(dimension_semantics=("arbitrary",),
                                             vmem_limit_bytes=VMEM_LIMIT_BYTES),
        name="ffn1_in_proj",
    )(x, pre_g, post_g, wgu, wd, mix_g, w_in)


def _ffn_ple_call(x, p, pre_g, post_g, wgu, wd, ple_g, w_gate, w_proj, ple_post_g):
    n = x.shape[0]
    tile = min(TOKEN_TILE, n)
    return pl.pallas_call(
        _ffn_ple_kernel,
        grid=(n // tile,),
        in_specs=[_row_spec(tile, D_MODEL), _row_spec(tile, D_PLE),
                  _const_spec(pre_g.shape), _const_spec(post_g.shape),
                  _const_spec(wgu.shape), _const_spec(wd.shape), _const_spec(ple_g.shape),
                  _const_spec(w_gate.shape), _const_spec(w_proj.shape), _const_spec(ple_post_g.shape)],
        out_specs=_row_spec(tile, D_MODEL),
        out_shape=jax.ShapeDtypeStruct((n, D_MODEL), _f32),
        scratch_shapes=[pltpu.VMEM((tile, D_FF), _bf16)],
        compiler_params=pltpu.CompilerParams(dimension_semantics=("arbitrary",),
                                             vmem_limit_bytes=VMEM_LIMIT_BYTES),
        name="ffn2_ple",
    )(x, p, pre_g, post_g, wgu, wd, ple_g, w_gate, w_proj, ple_post_g)


def _lru_gates(xc, w_gates, ba, bx):
    gates = _dot(xc.astype(_bf16), w_gates)
    return jax.nn.sigmoid(gates[:, :D_LRU] + ba), jax.nn.sigmoid(gates[:, D_LRU:] + bx)


def _lru_ab(xc, r, i_g, lam):
    neg_lam = -lam
    softplus = jnp.maximum(neg_lam, 0.0) + jnp.log1p(jnp.exp(-jnp.abs(neg_lam)))
    log_a = -LRU_C * r * softplus
    a = jnp.exp(log_a)
    b = jnp.sqrt(-jnp.tanh(log_a) * (a * a + 1.0)) * (i_g * xc)
    return a, b


def _causal_taps(buf, r0, c0, rows, pad, width, w_ref, b_ref):
    first = pad - (width - 1)
    x = buf[r0:r0 + rows + pad, c0:c0 + LANES]
    acc = jnp.broadcast_to(b_ref[:, c0:c0 + LANES], (rows, LANES))
    for phase in range(SUBLANES):
        taps = [(q, q * SUBLANES + phase - first) for q in range(pad // SUBLANES + 1)]
        taps = [(q, k) for q, k in taps if 0 <= k < width]
        if not taps:
            continue
        xs = x if phase == 0 else pltpu.roll(x, rows + pad - phase, axis=0)
        for q, k in taps:
            acc = acc + w_ref[k:k + 1, c0:c0 + LANES] * xs[q * SUBLANES:q * SUBLANES + rows, :]
    return acc


def _anchored(value, chains, which):
    pieces = [p for c in which for p in next(chains[c])]
    if not pieces:
        return value
    folded = []
    for p in pieces:
        m = jnp.min(p.reshape(-1, SUBLANES, p.shape[-1]), axis=0)
        folded += [m[:, c0:c0 + LANES] for c0 in range(0, m.shape[1], LANES)]
    zero = jnp.minimum(jnp.abs(functools.reduce(jnp.minimum, folded)), 0.0)
    zero = jnp.concatenate([zero] * (ANCHOR_ROWS // SUBLANES), axis=0).astype(value.dtype)
    zero = jnp.concatenate([zero] * (value.shape[1] // LANES), axis=1)
    split = value.shape[0] - ANCHOR_ROWS
    return jnp.concatenate([value[0:split], value[split:] + zero], axis=0)


def _ffn_anchored(x, pre_g, post_g, wgu_ref, wd_ref, act_ref, chains, per_ff, per_down):
    h = _rms(x, pre_g).astype(_bf16)
    for lo, which in zip(range(0, D_FF, FF_COLS), per_ff, strict=True):
        gate = _dot(h, wgu_ref[:, lo:lo + FF_COLS])
        up = _dot(h, wgu_ref[:, D_FF + lo:D_FF + lo + FF_COLS])
        act_ref[:, lo:lo + FF_COLS] = _anchored(gate * jax.nn.sigmoid(gate) * up, chains, which).astype(_bf16)
    ys = [_dot(_anchored(act_ref[...], chains, which), wd_ref[:, lo:lo + DOWN_COLS])
          for lo, which in zip(range(0, D_MODEL, DOWN_COLS), per_down, strict=True)]
    return x + FFN_RES_WEIGHT * _rms(jnp.concatenate(ys, axis=1), post_g)


def _lru_steps(fresh, lconv_w, lconv_b, w_gates, ba, bx, lam, ubuf, lconv_st_ref, h_st_ref,
               xbuf, xcbuf, hbuf, ybuf, hcar):
    tt = ubuf.shape[0]
    parts = [(lo, lo + PART_ROWS) for lo in range(0, tt, PART_ROWS)]

    for lo, hi in parts:
        blocks = []
        for r0 in range(lo, hi, CONV_ROWS):
            for c0 in range(0, D_LRU, LANES):
                xc = _causal_taps(xbuf, r0, c0, CONV_ROWS, LCONV_PAD, LRU_CONV_WIDTH, lconv_w, lconv_b)
                xcbuf[r0:r0 + CONV_ROWS, c0:c0 + LANES] = xc
                blocks.append(xc)
        yield blocks
    lconv_st_ref[0] = xbuf[pl.ds(LCONV_PAD + tt - (LRU_CONV_WIDTH - 1), LRU_CONV_WIDTH - 1), :]

    gates = []
    for lo, hi in parts:
        xc = xcbuf[lo:hi, :]
        r, i_g = _lru_gates(xc, w_gates[...], ba[...], bx[...])
        gates.append((xc, r, i_g))
        yield [r, i_g]
    coeffs = []
    for xc, r, i_g in gates:
        a, b = _lru_ab(xc, r, i_g, lam[...])
        coeffs.append((a, b))
        yield [b]

    scanned = []
    for a, b in coeffs:
        for lo in range(0, PART_ROWS, CONV_ROWS):
            aq = a[lo:lo + CONV_ROWS].reshape(CONV_ROWS // SUBLANES, SUBLANES, D_LRU)
            bq = b[lo:lo + CONV_ROWS].reshape(aq.shape)
            row = lax.broadcasted_iota(jnp.int32, aq.shape, 1)
            shift = 1
            while shift < SUBLANES:
                keep = row >= shift
                bq = jnp.where(keep, aq * pltpu.roll(bq, shift, axis=1) + bq, bq)
                aq = jnp.where(keep, aq * pltpu.roll(aq, shift, axis=1), aq)
                shift *= 2
            scanned.append((aq, bq))
            yield [aq, bq]

    carry = jnp.where(fresh, 0.0, hcar[...])
    r0 = 0
    for aq, bq in scanned:
        for i in range(aq.shape[0]):
            rows = aq[i] * carry + bq[i]
            hbuf[r0:r0 + SUBLANES, :] = rows
            carry = rows[SUBLANES - 1:SUBLANES, :]
            r0 += SUBLANES
    hcar[...] = carry
    h_st_ref[0] = carry
    yield [rows]

    for lo, hi in parts:
        yb = hbuf[lo:hi, :] * jax.nn.gelu(ubuf[lo:hi, 2 * D_CONV + D_LRU:D_IN])
        ybuf[lo:hi, :] = yb.astype(_bf16)
        yield [yb]


def _conv_steps(conv_w, conv_b, conv_g, ubuf, conv_st_ref, gbuf, cbuf):
    tt = cbuf.shape[0]
    for lo in range(0, tt, PART_ROWS):
        g = ubuf[lo:lo + PART_ROWS, 0:D_CONV] * jax.nn.sigmoid(ubuf[lo:lo + PART_ROWS, D_CONV:2 * D_CONV])
        gbuf[CONV_PAD + lo:CONV_PAD + lo + PART_ROWS, :] = g
        yield [g]
    conv_st_ref[0] = gbuf[pl.ds(CONV_PAD + tt - (CONV_WIDTH - 1), CONV_WIDTH - 1), :]
    for r0 in range(0, tt, CONV_ROWS):
        cols = []
        for c0 in range(0, D_CONV, LANES):
            cols.append(_causal_taps(gbuf, r0, c0, CONV_ROWS, CONV_PAD, CONV_WIDTH, conv_w, conv_b))
            if len(cols) * LANES < D_CONV:
                yield [cols[-1]]
        cn = _rms(jnp.concatenate(cols, axis=1), conv_g[...])
        c = cn * jax.nn.sigmoid(cn)
        cbuf[r0:r0 + CONV_ROWS, :] = c.astype(_bf16)
        yield [c]


def _out_steps(w_out, post_g, x2_ref, cbuf, ybuf):
    out = _dot(cbuf[...], w_out[0:D_CONV, :]) + _dot(ybuf[...], w_out[D_CONV:D_CONV + D_LRU, :])
    delta = _rms(out, post_g[...])
    x2_ref[...] += delta
    yield [delta]


def _prompt_a_kernel(x_ref, pre_g, post_g, wgu, wd, mix_g, w_in,
                     conv_w, conv_b, conv_g, lconv_w, lconv_b, w_gates, ba, bx, lam, w_out, mix_post_g,
                     x2_ref, conv_st_ref, lconv_st_ref, h_st_ref,
                     act_ref, ubuf, x1buf, gbuf, xbuf, cbuf, xcbuf, hbuf, ybuf, hcar, *, tiles_per_seq):
    s = pl.program_id(0)
    tt = x_ref.shape[0]

    @pl.when(s == 0)
    def _():
        ubuf[...] = jnp.zeros(ubuf.shape, _f32)
        x1buf[...] = jnp.zeros(x1buf.shape, _f32)
        gbuf[...] = jnp.zeros(gbuf.shape, _f32)
        xbuf[...] = jnp.zeros(xbuf.shape, _f32)
        hcar[...] = jnp.zeros(hcar.shape, _f32)

    fresh = lax.rem(s + tiles_per_seq - 1, tiles_per_seq) == 0
    x2_ref[...] = x1buf[...]
    gbuf[0:CONV_PAD, :] = jnp.where(fresh, 0.0, gbuf[tt:tt + CONV_PAD, :])
    xbuf[0:LCONV_PAD, :] = jnp.where(fresh, 0.0, xbuf[tt:tt + LCONV_PAD, :])
    xbuf[LCONV_PAD:LCONV_PAD + tt, :] = ubuf[:, 2 * D_CONV:2 * D_CONV + D_LRU]
    chains = {
        "L": _lru_steps(fresh, lconv_w, lconv_b, w_gates, ba, bx, lam, ubuf, lconv_st_ref, h_st_ref,
                        xbuf, xcbuf, hbuf, ybuf, hcar),
        "C": _conv_steps(conv_w, conv_b, conv_g, ubuf, conv_st_ref, gbuf, cbuf),
        "F": _out_steps(w_out, mix_post_g, x2_ref, cbuf, ybuf),
    }

    x1 = _ffn_anchored(x_ref[...], pre_g[...], post_g[...], wgu, wd, act_ref, chains, PIECES_PER_FF,
                       PIECES_PER_DOWN)
    x1buf[...] = x1
    hm = _rms(x1, mix_g[...]).astype(_bf16)
    for lo, which in zip(range(0, D_IN, IN_COLS), PIECES_PER_IN, strict=True):
        ubuf[:, lo:lo + IN_COLS] = _anchored(_dot(hm, w_in[:, lo:lo + IN_COLS]), chains, which)
    assert all(next(chain, None) is None for chain in chains.values())


def _prompt_a_call(x, bsz, pre_g, post_g, wgu, wd, mix_g, w_in, conv_w, conv_b, conv_g, lconv_w, lconv_b,
                   w_gates, ba, bx, lam, w_out, mix_post_g):
    n = x.shape[0]
    tt = TIME_TILE
    tiles = n // tt
    tiles_per_seq = tiles // bsz
    consts = (pre_g, post_g, wgu, wd, mix_g, w_in, conv_w, conv_b, conv_g, lconv_w, lconv_b, w_gates, ba, bx,
              lam, w_out, mix_post_g)
    prev_seq = lambda s: (jnp.maximum(s - 1, 0) // tiles_per_seq, 0, 0)
    return pl.pallas_call(
        functools.partial(_prompt_a_kernel, tiles_per_seq=tiles_per_seq),
        grid=(tiles + 1,),
        in_specs=[pl.BlockSpec((tt, D_MODEL), lambda s: (jnp.minimum(s, tiles - 1), 0))]
                 + [_const_spec(c.shape) for c in consts],
        out_specs=[pl.BlockSpec((tt, D_MODEL), lambda s: (jnp.maximum(s - 1, 0), 0)),
                   pl.BlockSpec((1, CONV_WIDTH - 1, D_CONV), prev_seq),
                   pl.BlockSpec((1, LRU_CONV_WIDTH - 1, D_LRU), prev_seq),
                   pl.BlockSpec((1, 1, D_LRU), prev_seq)],
        out_shape=[jax.ShapeDtypeStruct((n, D_MODEL), _f32),
                   jax.ShapeDtypeStruct((bsz, CONV_WIDTH - 1, D_CONV), _f32),
                   jax.ShapeDtypeStruct((bsz, LRU_CONV_WIDTH - 1, D_LRU), _f32),
                   jax.ShapeDtypeStruct((bsz, 1, D_LRU), _f32)],
        scratch_shapes=[pltpu.VMEM((tt, D_FF), _bf16),
                        pltpu.VMEM((tt, D_IN), _f32),
                        pltpu.VMEM((tt, D_MODEL), _f32),
                        pltpu.VMEM((CONV_PAD + tt, D_CONV), _f32),
                        pltpu.VMEM((LCONV_PAD + tt, D_LRU), _f32),
                        pltpu.VMEM((tt, D_CONV), _bf16),
                        pltpu.VMEM((tt, D_LRU), _f32),
                        pltpu.VMEM((tt, D_LRU), _f32),
                        pltpu.VMEM((tt, D_LRU), _bf16),
                        pltpu.VMEM((1, D_LRU), _f32)],
        compiler_params=pltpu.CompilerParams(dimension_semantics=("arbitrary",),
                                             vmem_limit_bytes=VMEM_LIMIT_BYTES),
        name="prompt_ffn1_mix",
    )(x, *consts)


def _mix_sample_kernel(u_ref, x1_ref, cache_ref, lstate_ref, h0_ref,
                       conv_w, conv_b, conv_g, lconv_w, lconv_b, w_gates, ba, bx, lam, w_out, post_g,
                       x2_ref, conv_st_ref, lconv_st_ref, h_st_ref):
    steps = u_ref.shape[0]
    past = CONV_WIDTH - 1
    lpast = LRU_CONV_WIDTH - 1

    gp = [cache_ref[j] for j in range(past)]
    gp += [u_ref[t, :, 0:D_CONV] * jax.nn.sigmoid(u_ref[t, :, D_CONV:2 * D_CONV]) for t in range(steps)]
    for j in range(past):
        conv_st_ref[j] = gp[j + steps]
    c_rows = []
    for t in range(steps):
        acc = jnp.broadcast_to(conv_b[...], gp[0].shape)
        for k in range(CONV_WIDTH):
            acc = acc + conv_w[k:k + 1, :] * gp[t + k]
        cn = _rms(acc, conv_g[...])
        c_rows.append(cn * jax.nn.sigmoid(cn))
    c = jnp.concatenate(c_rows, axis=0).astype(_bf16)

    xp = [lstate_ref[j] for j in range(lpast)]
    xp += [u_ref[t, :, 2 * D_CONV:2 * D_CONV + D_LRU] for t in range(steps)]
    for j in range(lpast):
        lconv_st_ref[j] = xp[j + steps]
    xc_rows = []
    for t in range(steps):
        acc = jnp.broadcast_to(lconv_b[...], xp[0].shape)
        for k in range(LRU_CONV_WIDTH):
            acc = acc + lconv_w[k:k + 1, :] * xp[t + k]
        xc_rows.append(acc)
    xc = jnp.concatenate(xc_rows, axis=0)
    r, i_g = _lru_gates(xc, w_gates[...], ba[...], bx[...])
    a, b = _lru_ab(xc, r, i_g, lam[...])
    bt = h0_ref.shape[0]
    h = h0_ref[...]
    hs = []
    for t in range(steps):
        h = a[t * bt:(t + 1) * bt, :] * h + b[t * bt:(t + 1) * bt, :]
        hs.append(h)
    h_st_ref[...] = h
    u_gelu = jnp.concatenate([u_ref[t, :, 2 * D_CONV + D_LRU:D_IN] for t in range(steps)], axis=0)
    yb = (jnp.concatenate(hs, axis=0) * jax.nn.gelu(u_gelu)).astype(_bf16)

    out = _dot(c, w_out[0:D_CONV, :]) + _dot(yb, w_out[D_CONV:D_CONV + D_LRU, :])
    x2 = _rms(out, post_g[...])
    for t in range(steps):
        x2_ref[t] = x1_ref[t] + x2[t * bt:(t + 1) * bt, :]


def _mix_sample_call(u, x1, cache, lstate, h0, conv_w, conv_b, conv_g, lconv_w, lconv_b, w_gates, ba, bx,
                     lam, w_out, post_g):
    steps, bsz, _ = u.shape
    bt = SAMPLE_BATCH_TILE
    consts = (conv_w, conv_b, conv_g, lconv_w, lconv_b, w_gates, ba, bx, lam, w_out, post_g)
    return pl.pallas_call(
        _mix_sample_kernel,
        grid=(bsz // bt,),
        in_specs=[pl.BlockSpec((steps, bt, D_IN), lambda i: (0, i, 0)),
                  pl.BlockSpec((steps, bt, D_MODEL), lambda i: (0, i, 0)),
                  pl.BlockSpec((CONV_WIDTH - 1, bt, D_CONV), lambda i: (0, i, 0)),
                  pl.BlockSpec((LRU_CONV_WIDTH - 1, bt, D_LRU), lambda i: (0, i, 0)),
                  pl.BlockSpec((bt, D_LRU), lambda i: (i, 0))]
                 + [_const_spec(c.shape) for c in consts],
        out_specs=[pl.BlockSpec((steps, bt, D_MODEL), lambda i: (0, i, 0)),
                   pl.BlockSpec((CONV_WIDTH - 1, bt, D_CONV), lambda i: (0, i, 0)),
                   pl.BlockSpec((LRU_CONV_WIDTH - 1, bt, D_LRU), lambda i: (0, i, 0)),
                   pl.BlockSpec((bt, D_LRU), lambda i: (i, 0))],
        out_shape=[jax.ShapeDtypeStruct((steps, bsz, D_MODEL), _f32),
                   jax.ShapeDtypeStruct((CONV_WIDTH - 1, bsz, D_CONV), _f32),
                   jax.ShapeDtypeStruct((LRU_CONV_WIDTH - 1, bsz, D_LRU), _f32),
                   jax.ShapeDtypeStruct((bsz, D_LRU), _f32)],
        compiler_params=pltpu.CompilerParams(dimension_semantics=("arbitrary",),
                                             vmem_limit_bytes=VMEM_LIMIT_BYTES),
        name="mix_sample",
    )(u, x1, cache, lstate, h0, *consts)


def _block_diag(w):
    heads, d, _ = w.shape
    tiled = jnp.tile(w.reshape(heads * d, d), (1, heads))
    rows = lax.broadcasted_iota(jnp.int32, tiled.shape, 0) // d
    cols = lax.broadcasted_iota(jnp.int32, tiled.shape, 1) // d
    return jnp.where(rows == cols, tiled, 0.0)


def kernel(x_prompt, x_sample, cache_conv, state_lru_conv, state_lru_h, p_prompt, p_sample, ffn1_pre_g, ffn1_post_g, ffn1_w_gu, ffn1_w_down, mix_pre_g, mix_post_g, w_in, conv_w, conv_b, conv_norm_g, lru_conv_w, lru_conv_b, lru_wa, lru_ba, lru_wx, lru_bx, lru_lambda, w_out, ffn2_pre_g, ffn2_post_g, ffn2_w_gu, ffn2_w_down, ple_norm_g, ple_w_gate, ple_w_proj, ple_post_g):
    depth = ffn1_w_gu.shape[0]
    bsz, seq, _ = x_prompt.shape
    dbsz, dseq, _ = x_sample.shape

    xp = x_prompt.reshape(bsz * seq, D_MODEL)
    xs = x_sample.transpose(1, 0, 2).reshape(dseq * dbsz, D_MODEL)
    outs = [[] for _ in range(6)]
    for l in range(depth):
        row = lambda v: v[l:l + 1]
        wgu1, wd1 = ffn1_w_gu[l].astype(_bf16), ffn1_w_down[l].astype(_bf16)
        wgu2, wd2 = ffn2_w_gu[l].astype(_bf16), ffn2_w_down[l].astype(_bf16)
        w_in_l, w_out_l = w_in[l].astype(_bf16), w_out[l].astype(_bf16)
        w_gate_l, w_proj_l = ple_w_gate[l].astype(_bf16), ple_w_proj[l].astype(_bf16)
        w_gates = jnp.concatenate([_block_diag(lru_wa[l]), _block_diag(lru_wx[l])], axis=1).astype(_bf16)
        conv_consts = (conv_w[l], row(conv_b), row(conv_norm_g))
        lru_consts = (lru_conv_w[l], row(lru_conv_b), w_gates, row(lru_ba), row(lru_bx), row(lru_lambda))
        ffn1_args = (row(ffn1_pre_g), row(ffn1_post_g), wgu1, wd1, row(mix_pre_g), w_in_l)
        ffn2_args = (row(ffn2_pre_g), row(ffn2_post_g), wgu2, wd2, row(ple_norm_g), w_gate_l, w_proj_l,
                     row(ple_post_g))

        x2, cst, lst, hst = _prompt_a_call(xp, bsz, *ffn1_args, *conv_consts, *lru_consts, w_out_l,
                                           row(mix_post_g))
        xp = _ffn_ple_call(x2, p_prompt[l].reshape(bsz * seq, D_PLE), *ffn2_args)
        outs[0].append(cst); outs[1].append(lst); outs[2].append(hst.reshape(bsz, D_LRU))

        x1, u = _ffn_in_call(xs, *ffn1_args)
        x2, cst, lst, hst = _mix_sample_call(u.reshape(dseq, dbsz, D_IN), x1.reshape(dseq, dbsz, D_MODEL),
                                             cache_conv[l].transpose(1, 0, 2), state_lru_conv[l].transpose(1, 0, 2),
                                             state_lru_h[l], *conv_consts,
                                             *lru_consts, w_out_l, row(mix_post_g))
        ps = p_sample[l].transpose(1, 0, 2).reshape(dseq * dbsz, D_PLE)
        xs = _ffn_ple_call(x2.reshape(dseq * dbsz, D_MODEL), ps, *ffn2_args)
        outs[3].append(cst.transpose(1, 0, 2)); outs[4].append(lst.transpose(1, 0, 2)); outs[5].append(hst)

    y_prompt = xp.reshape(bsz, seq, D_MODEL)
    y_sample = xs.reshape(dseq, dbsz, D_MODEL).transpose(1, 0, 2)
    return (y_prompt, y_sample) + tuple(o[0][None] if depth == 1 else jnp.stack(o) for o in outs)
```

```python
import functools

import jax
import jax.numpy as jnp
from jax import lax
from jax.experimental import pallas as pl
from jax.experimental.pallas import tpu as pltpu

D_MODEL = 1024
D_CONV = 512
D_LRU = 512
D_IN = 2 * D_CONV + 2 * D_LRU
D_FF = 2816
D_PLE = 256
LRU_HEADS = 8
CONV_WIDTH = 31
LRU_CONV_WIDTH = 4
LRU_C = 8.0
EPS = 1e-6
FFN_RES_WEIGHT = 0.5

SUBLANES = 8
LANES = 128
VMEM_LIMIT_BYTES = 56 * 1024 * 1024

TOKEN_TILE = 1024
TIME_TILE = 512
CONV_ROWS = 64
CONV_PAD = 32
LCONV_PAD = 8
SAMPLE_BATCH_TILE = 64
ANCHOR_ROWS = 16

FF_CHUNKS = tuple((lo, min(512, D_FF - lo)) for lo in range(0, D_FF, 512))
STREAM_COLS = D_FF // 2

FF_COLS = 256
DOWN_COLS = 256
IN_COLS = 512
PART_ROWS = 128
PIECES_PER_FF = ("CCCC", "LC", "LC", "LC", "LC", "LC", "LC", "LC", "LC", "LC", "LC")
PIECES_PER_DOWN = ("", "LLCCC", "LLCCC", "LLCC")
PIECES_PER_IN = ("LLLLCCCC", "LCCCCCC", "LLLLCCCC", "F")

_bf16 = jnp.bfloat16
_f32 = jnp.float32


def _rms(x, g):
    ms = jnp.mean(x * x, axis=-1, keepdims=True)
    return x * lax.rsqrt(ms + EPS) * g


def _dot(a, b):
    return jnp.dot(a, b, preferred_element_type=_f32)


def _ffn(x, pre_g, post_g, wgu_ref, wd_ref, act_ref):
    h = _rms(x, pre_g).astype(_bf16)
    for lo, n in FF_CHUNKS:
        gate = _dot(h, wgu_ref[:, lo:lo + n])
        up = _dot(h, wgu_ref[:, D_FF + lo:D_FF + lo + n])
        act_ref[:, lo:lo + n] = (gate * jax.nn.sigmoid(gate) * up).astype(_bf16)
    y = _dot(act_ref[...], wd_ref[...])
    return x + FFN_RES_WEIGHT * _rms(y, post_g)


def _ffn_streamed(x_ref, pre_g, post_g, wg_ref, wu_ref, wd_ref, hbuf, act_ref, acc, finish):
    j = pl.program_id(0)

    @pl.when(j == 0)
    def _():
        hbuf[...] = _rms(x_ref[...], pre_g[...]).astype(_bf16)

    h = hbuf[...]
    for lo in range(0, STREAM_COLS, 512):
        n = min(512, STREAM_COLS - lo)
        gate = _dot(h, wg_ref[:, lo:lo + n])
        up = _dot(h, wu_ref[:, lo:lo + n])
        act_ref[:, lo:lo + n] = (gate * jax.nn.sigmoid(gate) * up).astype(_bf16)
    y = _dot(act_ref[...], wd_ref[...])

    @pl.when(j == 0)
    def _():
        acc[...] = y

    @pl.when(j > 0)
    def _():
        acc[...] += y

    @pl.when(j == pl.num_programs(0) - 1)
    def _():
        finish(x_ref[...] + FFN_RES_WEIGHT * _rms(acc[...], post_g[...]))


def _ffn_in_kernel(x_ref, pre_g, post_g, wg, wu, wd, mix_g, w_in, x1_ref, u_ref, hbuf, act_ref, acc):
    def finish(x1):
        x1_ref[...] = x1
        u_ref[...] = _dot(_rms(x1, mix_g[...]).astype(_bf16), w_in[...])

    _ffn_streamed(x_ref, pre_g, post_g, wg, wu, wd, hbuf, act_ref, acc, finish)


def _ffn_ple_streamed_kernel(x_ref, p_ref, pre_g, post_g, wg, wu, wd, ple_g, w_gate, w_proj, ple_post_g,
                             y_ref, hbuf, act_ref, acc):
    def finish(x3):
        gate = jax.nn.sigmoid(_dot(_rms(x3, ple_g[...]).astype(_bf16), w_gate[...]))
        e = _dot(p_ref[...].astype(_bf16), w_proj[...])
        y_ref[...] = x3 + _rms(gate * e, ple_post_g[...])

    _ffn_streamed(x_ref, pre_g, post_g, wg, wu, wd, hbuf, act_ref, acc, finish)


def _ffn_ple_kernel(x_ref, p_ref, pre_g, post_g, wgu, wd, ple_g, w_gate, w_proj, ple_post_g,
                    y_ref, act_ref):
    x3 = _ffn(x_ref[...], pre_g[...], post_g[...], wgu, wd, act_ref)
    gate = jax.nn.sigmoid(_dot(_rms(x3, ple_g[...]).astype(_bf16), w_gate[...]))
    e = _dot(p_ref[...].astype(_bf16), w_proj[...])
    y_ref[...] = x3 + _rms(gate * e, ple_post_g[...])


def _const_spec(shape):
    zeros = (0,) * len(shape)
    return pl.BlockSpec(shape, lambda *_: zeros, pipeline_mode=pl.Buffered(1))


def _row_spec(tile, width):
    return pl.BlockSpec((tile, width), lambda i: (i, 0))


def _stream_specs(n):
    chunks = D_FF // STREAM_COLS
    tokens = lambda width: pl.BlockSpec((n, width), lambda j: (0, 0))
    weights = [pl.BlockSpec((D_MODEL, STREAM_COLS), lambda j: (0, j)),
               pl.BlockSpec((D_MODEL, STREAM_COLS), lambda j: (0, j + chunks)),
               pl.BlockSpec((STREAM_COLS, D_MODEL), lambda j: (j, 0))]
    scratch = [pltpu.VMEM((n, D_MODEL), _bf16), pltpu.VMEM((n, STREAM_COLS), _bf16),
               pltpu.VMEM((n, D_MODEL), _f32)]
    return chunks, tokens, weights, scratch


def _ffn_in_call(x, pre_g, post_g, wgu, wd, mix_g, w_in):
    n = x.shape[0]
    chunks, tokens, weights, scratch = _stream_specs(n)
    return pl.pallas_call(
        _ffn_in_kernel,
        grid=(chunks,),
        in_specs=[tokens(D_MODEL), _const_spec(pre_g.shape), _const_spec(post_g.shape), *weights,
                  _const_spec(mix_g.shape), _const_spec(w_in.shape)],
        out_specs=[tokens(D_MODEL), tokens(D_IN)],
        out_shape=[jax.ShapeDtypeStruct((n, D_MODEL), _f32), jax.ShapeDtypeStruct((n, D_IN), _f32)],
        scratch_shapes=scratch,
        compiler_params=pltpu.CompilerParams(dimension_semantics=("arbitrary",),
                                             vmem_limit_bytes=VMEM_LIMIT_BYTES),
        name="ffn1_in_proj",
    )(x, pre_g, post_g, wgu, wgu, wd, mix_g, w_in)


def _ffn_ple_streamed_call(x, p, pre_g, post_g, wgu, wd, ple_g, w_gate, w_proj, ple_post_g):
    n = x.shape[0]
    chunks, tokens, weights, scratch = _stream_specs(n)
    return pl.pallas_call(
        _ffn_ple_streamed_kernel,
        grid=(chunks,),
        in_specs=[tokens(D_MODEL), tokens(D_PLE), _const_spec(pre_g.shape), _const_spec(post_g.shape), *weights,
                  _const_spec(ple_g.shape), _const_spec(w_gate.shape), _const_spec(w_proj.shape),
                  _const_spec(ple_post_g.shape)],
        out_specs=tokens(D_MODEL),
        out_shape=jax.ShapeDtypeStruct((n, D_MODEL), _f32),
        scratch_shapes=scratch,
        compiler_params=pltpu.CompilerParams(dimension_semantics=("arbitrary",),
                                             vmem_limit_bytes=VMEM_LIMIT_BYTES),
        name="ffn2_ple_sample",
    )(x, p, pre_g, post_g, wgu, wgu, wd, ple_g, w_gate, w_proj, ple_post_g)


def _ffn_ple_call(x, p, pre_g, post_g, wgu, wd, ple_g, w_gate, w_proj, ple_post_g):
    n = x.shape[0]
    tile = min(TOKEN_TILE, n)
    return pl.pallas_call(
        _ffn_ple_kernel,
        grid=(n // tile,),
        in_specs=[_row_spec(tile, D_MODEL), _row_spec(tile, D_PLE),
                  _const_spec(pre_g.shape), _const_spec(post_g.shape),
                  _const_spec(wgu.shape), _const_spec(wd.shape), _const_spec(ple_g.shape),
                  _const_spec(w_gate.shape), _const_spec(w_proj.shape), _const_spec(ple_post_g.shape)],
        out_specs=_row_spec(tile, D_MODEL),
        out_shape=jax.ShapeDtypeStruct((n, D_MODEL), _f32),
        scratch_shapes=[pltpu.VMEM((tile, D_FF), _bf16)],
        compiler_params=pltpu.CompilerParams(dimension_semantics=("arbitrary",),
                                             vmem_limit_bytes=VMEM_LIMIT_BYTES),
        name="ffn2_ple",
    )(x, p, pre_g, post_g, wgu, wd, ple_g, w_gate, w_proj, ple_post_g)


def _lru_gates(xc, w_gates, ba, bx):
    gates = _dot(xc.astype(_bf16), w_gates)
    return jax.nn.sigmoid(gates[:, :D_LRU] + ba), jax.nn.sigmoid(gates[:, D_LRU:] + bx)


def _lru_ab(xc, r, i_g, lam):
    neg_lam = -lam
    softplus = jnp.maximum(neg_lam, 0.0) + jnp.log1p(jnp.exp(-jnp.abs(neg_lam)))
    log_a = -LRU_C * r * softplus
    a = jnp.exp(log_a)
    b = jnp.sqrt(-jnp.tanh(log_a) * (a * a + 1.0)) * (i_g * xc)
    return a, b


def _causal_taps(buf, r0, c0, rows, pad, width, w_ref, b_ref):
    first = pad - (width - 1)
    x = buf[r0:r0 + rows + pad, c0:c0 + LANES]
    acc = jnp.broadcast_to(b_ref[:, c0:c0 + LANES], (rows, LANES))
    for phase in range(SUBLANES):
        taps = [(q, q * SUBLANES + phase - first) for q in range(pad // SUBLANES + 1)]
        taps = [(q, k) for q, k in taps if 0 <= k < width]
        if not taps:
            continue
        xs = x if phase == 0 else pltpu.roll(x, rows + pad - phase, axis=0)
        for q, k in taps:
            acc = acc + w_ref[k:k + 1, c0:c0 + LANES] * xs[q * SUBLANES:q * SUBLANES + rows, :]
    return acc


def _anchored(value, chains, which):
    pieces = [p for c in which for p in next(chains[c])]
    if not pieces:
        return value
    folded = []
    for p in pieces:
        m = jnp.min(p.reshape(-1, SUBLANES, p.shape[-1]), axis=0)
        folded += [m[:, c0:c0 + LANES] for c0 in range(0, m.shape[1], LANES)]
    zero = jnp.minimum(jnp.abs(functools.reduce(jnp.minimum, folded)), 0.0)
    zero = jnp.concatenate([zero] * (ANCHOR_ROWS // SUBLANES), axis=0).astype(value.dtype)
    zero = jnp.concatenate([zero] * (value.shape[1] // LANES), axis=1)
    split = value.shape[0] - ANCHOR_ROWS
    return jnp.concatenate([value[0:split], value[split:] + zero], axis=0)


def _ffn_anchored(x, pre_g, post_g, wgu_ref, wd_ref, act_ref, chains, per_ff, per_down):
    h = _rms(x, pre_g).astype(_bf16)
    for lo, which in zip(range(0, D_FF, FF_COLS), per_ff, strict=True):
        gate = _dot(h, wgu_ref[:, lo:lo + FF_COLS])
        up = _dot(h, wgu_ref[:, D_FF + lo:D_FF + lo + FF_COLS])
        act_ref[:, lo:lo + FF_COLS] = _anchored(gate * jax.nn.sigmoid(gate) * up, chains, which).astype(_bf16)
    ys = [_dot(_anchored(act_ref[...], chains, which), wd_ref[:, lo:lo + DOWN_COLS])
          for lo, which in zip(range(0, D_MODEL, DOWN_COLS), per_down, strict=True)]
    return x + FFN_RES_WEIGHT * _rms(jnp.concatenate(ys, axis=1), post_g)


def _lru_steps(fresh, lconv_w, lconv_b, w_gates, ba, bx, lam, ubuf, lconv_st_ref, h_st_ref,
               xbuf, xcbuf, hbuf, ybuf, hcar):
    tt = ubuf.shape[0]
    parts = [(lo, lo + PART_ROWS) for lo in range(0, tt, PART_ROWS)]

    for lo, hi in parts:
        blocks = []
        for r0 in range(lo, hi, CONV_ROWS):
            for c0 in range(0, D_LRU, LANES):
                xc = _causal_taps(xbuf, r0, c0, CONV_ROWS, LCONV_PAD, LRU_CONV_WIDTH, lconv_w, lconv_b)
                xcbuf[r0:r0 + CONV_ROWS, c0:c0 + LANES] = xc
                blocks.append(xc)
        yield blocks
    lconv_st_ref[0] = xbuf[pl.ds(LCONV_PAD + tt - (LRU_CONV_WIDTH - 1), LRU_CONV_WIDTH - 1), :]

    gates = []
    for lo, hi in parts:
        xc = xcbuf[lo:hi, :]
        r, i_g = _lru_gates(xc, w_gates[...], ba[...], bx[...])
        gates.append((xc, r, i_g))
        yield [r, i_g]
    coeffs = []
    for xc, r, i_g in gates:
        a, b = _lru_ab(xc, r, i_g, lam[...])
        coeffs.append((a, b))
        yield [b]

    scanned = []
    for a, b in coeffs:
        for lo in range(0, PART_ROWS, CONV_ROWS):
            aq = a[lo:lo + CONV_ROWS].reshape(CONV_ROWS // SUBLANES, SUBLANES, D_LRU)
            bq = b[lo:lo + CONV_ROWS].reshape(aq.shape)
            row = lax.broadcasted_iota(jnp.int32, aq.shape, 1)
            shift = 1
            while shift < SUBLANES:
                keep = row >= shift
                bq = jnp.where(keep, aq * pltpu.roll(bq, shift, axis=1) + bq, bq)
                aq = jnp.where(keep, aq * pltpu.roll(aq, shift, axis=1), aq)
                shift *= 2
            scanned.append((aq, bq))
            yield [aq, bq]

    carry = jnp.where(fresh, 0.0, hcar[...])
    r0 = 0
    for aq, bq in scanned:
        for i in range(aq.shape[0]):
            rows = aq[i] * carry + bq[i]
            hbuf[r0:r0 + SUBLANES, :] = rows
            carry = rows[SUBLANES - 1:SUBLANES, :]
            r0 += SUBLANES
    hcar[...] = carry
    h_st_ref[0] = carry
    yield [rows]

    for lo, hi in parts:
        yb = hbuf[lo:hi, :] * jax.nn.gelu(ubuf[lo:hi, 2 * D_CONV + D_LRU:D_IN])
        ybuf[lo:hi, :] = yb.astype(_bf16)
        yield [yb]


def _conv_steps(conv_w, conv_b, conv_g, ubuf, conv_st_ref, gbuf, cbuf):
    tt = cbuf.shape[0]
    for lo in range(0, tt, PART_ROWS):
        g = ubuf[lo:lo + PART_ROWS, 0:D_CONV] * jax.nn.sigmoid(ubuf[lo:lo + PART_ROWS, D_CONV:2 * D_CONV])
        gbuf[CONV_PAD + lo:CONV_PAD + lo + PART_ROWS, :] = g
        yield [g]
    conv_st_ref[0] = gbuf[pl.ds(CONV_PAD + tt - (CONV_WIDTH - 1), CONV_WIDTH - 1), :]
    for r0 in range(0, tt, CONV_ROWS):
        cols = []
        for c0 in range(0, D_CONV, LANES):
            cols.append(_causal_taps(gbuf, r0, c0, CONV_ROWS, CONV_PAD, CONV_WIDTH, conv_w, conv_b))
            if len(cols) * LANES < D_CONV:
                yield [cols[-1]]
        cn = _rms(jnp.concatenate(cols, axis=1), conv_g[...])
        c = cn * jax.nn.sigmoid(cn)
        cbuf[r0:r0 + CONV_ROWS, :] = c.astype(_bf16)
        yield [c]


def _out_steps(w_out, post_g, x2_ref, cbuf, ybuf):
    out = _dot(cbuf[...], w_out[0:D_CONV, :]) + _dot(ybuf[...], w_out[D_CONV:D_CONV + D_LRU, :])
    delta = _rms(out, post_g[...])
    x2_ref[...] += delta
    yield [delta]


def _prompt_a_kernel(x_ref, pre_g, post_g, wgu, wd, mix_g, w_in,
                     conv_w, conv_b, conv_g, lconv_w, lconv_b, w_gates, ba, bx, lam, w_out, mix_post_g,
                     x2_ref, conv_st_ref, lconv_st_ref, h_st_ref,
                     act_ref, ubuf, x1buf, gbuf, xbuf, cbuf, xcbuf, hbuf, ybuf, hcar, *, tiles_per_seq):
    s = pl.program_id(0)
    tt = x_ref.shape[0]

    @pl.when(s == 0)
    def _():
        ubuf[...] = jnp.zeros(ubuf.shape, _f32)
        x1buf[...] = jnp.zeros(x1buf.shape, _f32)
        gbuf[...] = jnp.zeros(gbuf.shape, _f32)
        xbuf[...] = jnp.zeros(xbuf.shape, _f32)
        hcar[...] = jnp.zeros(hcar.shape, _f32)

    fresh = lax.rem(s + tiles_per_seq - 1, tiles_per_seq) == 0
    x2_ref[...] = x1buf[...]
    gbuf[0:CONV_PAD, :] = jnp.where(fresh, 0.0, gbuf[tt:tt + CONV_PAD, :])
    xbuf[0:LCONV_PAD, :] = jnp.where(fresh, 0.0, xbuf[tt:tt + LCONV_PAD, :])
    xbuf[LCONV_PAD:LCONV_PAD + tt, :] = ubuf[:, 2 * D_CONV:2 * D_CONV + D_LRU]
    chains = {
        "L": _lru_steps(fresh, lconv_w, lconv_b, w_gates, ba, bx, lam, ubuf, lconv_st_ref, h_st_ref,
                        xbuf, xcbuf, hbuf, ybuf, hcar),
        "C": _conv_steps(conv_w, conv_b, conv_g, ubuf, conv_st_ref, gbuf, cbuf),
        "F": _out_steps(w_out, mix_post_g, x2_ref, cbuf, ybuf),
    }

    x1 = _ffn_anchored(x_ref[...], pre_g[...], post_g[...], wgu, wd, act_ref, chains, PIECES_PER_FF,
                       PIECES_PER_DOWN)
    x1buf[...] = x1
    hm = _rms(x1, mix_g[...]).astype(_bf16)
    for lo, which in zip(range(0, D_IN, IN_COLS), PIECES_PER_IN, strict=True):
        ubuf[:, lo:lo + IN_COLS] = _anchored(_dot(hm, w_in[:, lo:lo + IN_COLS]), chains, which)
    assert all(next(chain, None) is None for chain in chains.values())


def _prompt_a_call(x, bsz, pre_g, post_g, wgu, wd, mix_g, w_in, conv_w, conv_b, conv_g, lconv_w, lconv_b,
                   w_gates, ba, bx, lam, w_out, mix_post_g):
    n = x.shape[0]
    tt = TIME_TILE
    tiles = n // tt
    tiles_per_seq = tiles // bsz
    consts = (pre_g, post_g, wgu, wd, mix_g, w_in, conv_w, conv_b, conv_g, lconv_w, lconv_b, w_gates, ba, bx,
              lam, w_out, mix_post_g)
    prev_seq = lambda s: (jnp.maximum(s - 1, 0) // tiles_per_seq, 0, 0)
    return pl.pallas_call(
        functools.partial(_prompt_a_kernel, tiles_per_seq=tiles_per_seq),
        grid=(tiles + 1,),
        in_specs=[pl.BlockSpec((tt, D_MODEL), lambda s: (jnp.minimum(s, tiles - 1), 0))]
                 + [_const_spec(c.shape) for c in consts],
        out_specs=[pl.BlockSpec((tt, D_MODEL), lambda s: (jnp.maximum(s - 1, 0), 0)),
                   pl.BlockSpec((1, CONV_WIDTH - 1, D_CONV), prev_seq),
                   pl.BlockSpec((1, LRU_CONV_WIDTH - 1, D_LRU), prev_seq),
                   pl.BlockSpec((1, 1, D_LRU), prev_seq)],
        out_shape=[jax.ShapeDtypeStruct((n, D_MODEL), _f32),
                   jax.ShapeDtypeStruct((bsz, CONV_WIDTH - 1, D_CONV), _f32),
                   jax.ShapeDtypeStruct((bsz, LRU_CONV_WIDTH - 1, D_LRU), _f32),
                   jax.ShapeDtypeStruct((bsz, 1, D_LRU), _f32)],
        scratch_shapes=[pltpu.VMEM((tt, D_FF), _bf16),
                        pltpu.VMEM((tt, D_IN), _f32),
                        pltpu.VMEM((tt, D_MODEL), _f32),
                        pltpu.VMEM((CONV_PAD + tt, D_CONV), _f32),
                        pltpu.VMEM((LCONV_PAD + tt, D_LRU), _f32),
                        pltpu.VMEM((tt, D_CONV), _bf16),
                        pltpu.VMEM((tt, D_LRU), _f32),
                        pltpu.VMEM((tt, D_LRU), _f32),
                        pltpu.VMEM((tt, D_LRU), _bf16),
                        pltpu.VMEM((1, D_LRU), _f32)],
        compiler_params=pltpu.CompilerParams(dimension_semantics=("arbitrary",),
                                             vmem_limit_bytes=VMEM_LIMIT_BYTES),
        name="prompt_ffn1_mix",
    )(x, *consts)


def _mix_sample_kernel(u_ref, x1_ref, cache_ref, lstate_ref, h0_ref,
                       conv_w, conv_b, conv_g, lconv_w, lconv_b, w_gates, ba, bx, lam, w_out, post_g,
                       x2_ref, conv_st_ref, lconv_st_ref, h_st_ref):
    steps = u_ref.shape[0]
    past = CONV_WIDTH - 1
    lpast = LRU_CONV_WIDTH - 1

    gp = [cache_ref[j] for j in range(past)]
    gp += [u_ref[t, :, 0:D_CONV] * jax.nn.sigmoid(u_ref[t, :, D_CONV:2 * D_CONV]) for t in range(steps)]
    for j in range(past):
        conv_st_ref[j] = gp[j + steps]
    c_rows = []
    for t in range(steps):
        acc = jnp.broadcast_to(conv_b[...], gp[0].shape)
        for k in range(CONV_WIDTH):
            acc = acc + conv_w[k:k + 1, :] * gp[t + k]
        cn = _rms(acc, conv_g[...])
        c_rows.append(cn * jax.nn.sigmoid(cn))
    c = jnp.concatenate(c_rows, axis=0).astype(_bf16)

    xp = [lstate_ref[j] for j in range(lpast)]
    xp += [u_ref[t, :, 2 * D_CONV:2 * D_CONV + D_LRU] for t in range(steps)]
    for j in range(lpast):
        lconv_st_ref[j] = xp[j + steps]
    xc_rows = []
    for t in range(steps):
        acc = jnp.broadcast_to(lconv_b[...], xp[0].shape)
        for k in range(LRU_CONV_WIDTH):
            acc = acc + lconv_w[k:k + 1, :] * xp[t + k]
        xc_rows.append(acc)
    xc = jnp.concatenate(xc_rows, axis=0)
    r, i_g = _lru_gates(xc, w_gates[...], ba[...], bx[...])
    a, b = _lru_ab(xc, r, i_g, lam[...])
    bt = h0_ref.shape[0]
    h = h0_ref[...]
    hs = []
    for t in range(steps):
        h = a[t * bt:(t + 1) * bt, :] * h + b[t * bt:(t + 1) * bt, :]
        hs.append(h)
    h_st_ref[...] = h
    u_gelu = jnp.concatenate([u_ref[t, :, 2 * D_CONV + D_LRU:D_IN] for t in range(steps)], axis=0)
    yb = (jnp.concatenate(hs, axis=0) * jax.nn.gelu(u_gelu)).astype(_bf16)

    out = _dot(c, w_out[0:D_CONV, :]) + _dot(yb, w_out[D_CONV:D_CONV + D_LRU, :])
    x2 = _rms(out, post_g[...])
    for t in range(steps):
        x2_ref[t] = x1_ref[t] + x2[t * bt:(t + 1) * bt, :]


def _mix_sample_call(u, x1, cache, lstate, h0, conv_w, conv_b, conv_g, lconv_w, lconv_b, w_gates, ba, bx,
                     lam, w_out, post_g):
    steps, bsz, _ = u.shape
    bt = SAMPLE_BATCH_TILE
    consts = (conv_w, conv_b, conv_g, lconv_w, lconv_b, w_gates, ba, bx, lam, w_out, post_g)
    return pl.pallas_call(
        _mix_sample_kernel,
        grid=(bsz // bt,),
        in_specs=[pl.BlockSpec((steps, bt, D_IN), lambda i: (0, i, 0)),
                  pl.BlockSpec((steps, bt, D_MODEL), lambda i: (0, i, 0)),
                  pl.BlockSpec((CONV_WIDTH - 1, bt, D_CONV), lambda i: (0, i, 0)),
                  pl.BlockSpec((LRU_CONV_WIDTH - 1, bt, D_LRU), lambda i: (0, i, 0)),
                  pl.BlockSpec((bt, D_LRU), lambda i: (i, 0))]
                 + [_const_spec(c.shape) for c in consts],
        out_specs=[pl.BlockSpec((steps, bt, D_MODEL), lambda i: (0, i, 0)),
                   pl.BlockSpec((CONV_WIDTH - 1, bt, D_CONV), lambda i: (0, i, 0)),
                   pl.BlockSpec((LRU_CONV_WIDTH - 1, bt, D_LRU), lambda i: (0, i, 0)),
                   pl.BlockSpec((bt, D_LRU), lambda i: (i, 0))],
        out_shape=[jax.ShapeDtypeStruct((steps, bsz, D_MODEL), _f32),
                   jax.ShapeDtypeStruct((CONV_WIDTH - 1, bsz, D_CONV), _f32),
                   jax.ShapeDtypeStruct((LRU_CONV_WIDTH - 1, bsz, D_LRU), _f32),
                   jax.ShapeDtypeStruct((bsz, D_LRU), _f32)],
        compiler_params=pltpu.CompilerParams(dimension_semantics=("arbitrary",),
                                             vmem_limit_bytes=VMEM_LIMIT_BYTES),
        name="mix_sample",
    )(u, x1, cache, lstate, h0, *consts)


def _block_diag(w):
    heads, d, _ = w.shape
    tiled = jnp.tile(w.reshape(heads * d, d), (1, heads))
    rows = lax.broadcasted_iota(jnp.int32, tiled.shape, 0) // d
    cols = lax.broadcasted_iota(jnp.int32, tiled.shape, 1) // d
    return jnp.where(rows == cols, tiled, 0.0)


def kernel(x_prompt, x_sample, cache_conv, state_lru_conv, state_lru_h, p_prompt, p_sample, ffn1_pre_g, ffn1_post_g, ffn1_w_gu, ffn1_w_down, mix_pre_g, mix_post_g, w_in, conv_w, conv_b, conv_norm_g, lru_conv_w, lru_conv_b, lru_wa, lru_ba, lru_wx, lru_bx, lru_lambda, w_out, ffn2_pre_g, ffn2_post_g, ffn2_w_gu, ffn2_w_down, ple_norm_g, ple_w_gate, ple_w_proj, ple_post_g):
    depth = ffn1_w_gu.shape[0]
    bsz, seq, _ = x_prompt.shape
    dbsz, dseq, _ = x_sample.shape

    xp = x_prompt.reshape(bsz * seq, D_MODEL)
    xs = x_sample.transpose(1, 0, 2).reshape(dseq * dbsz, D_MODEL)
    outs = [[] for _ in range(6)]
    for l in range(depth):
        row = lambda v: v[l:l + 1]
        wgu1, wd1 = ffn1_w_gu[l].astype(_bf16), ffn1_w_down[l].astype(_bf16)
        wgu2, wd2 = ffn2_w_gu[l].astype(_bf16), ffn2_w_down[l].astype(_bf16)
        w_in_l, w_out_l = w_in[l].astype(_bf16), w_out[l].astype(_bf16)
        w_gate_l, w_proj_l = ple_w_gate[l].astype(_bf16), ple_w_proj[l].astype(_bf16)
        w_gates = jnp.concatenate([_block_diag(lru_wa[l]), _block_diag(lru_wx[l])], axis=1).astype(_bf16)
        conv_consts = (conv_w[l], row(conv_b), row(conv_norm_g))
        lru_consts = (lru_conv_w[l], row(lru_conv_b), w_gates, row(lru_ba), row(lru_bx), row(lru_lambda))
        ffn1_args = (row(ffn1_pre_g), row(ffn1_post_g), wgu1, wd1, row(mix_pre_g), w_in_l)
        ffn2_args = (row(ffn2_pre_g), row(ffn2_post_g), wgu2, wd2, row(ple_norm_g), w_gate_l, w_proj_l,
                     row(ple_post_g))

        x2, cst, lst, hst = _prompt_a_call(xp, bsz, *ffn1_args, *conv_consts, *lru_consts, w_out_l,
                                           row(mix_post_g))
        xp = _ffn_ple_call(x2, p_prompt[l].reshape(bsz * seq, D_PLE), *ffn2_args)
        outs[0].append(cst); outs[1].append(lst); outs[2].append(hst.reshape(bsz, D_LRU))

        x1, u = _ffn_in_call(xs, *ffn1_args)
        x2, cst, lst, hst = _mix_sample_call(u.reshape(dseq, dbsz, D_IN), x1.reshape(dseq, dbsz, D_MODEL),
                                             cache_conv[l].transpose(1, 0, 2), state_lru_conv[l].transpose(1, 0, 2),
                                             state_lru_h[l], *conv_consts,
                                             *lru_consts, w_out_l, row(mix_post_g))
        ps = p_sample[l].transpose(1, 0, 2).reshape(dseq * dbsz, D_PLE)
        xs = _ffn_ple_streamed_call(x2.reshape(dseq * dbsz, D_MODEL), ps, *ffn2_args)
        outs[3].append(cst.transpose(1, 0, 2)); outs[4].append(lst.transpose(1, 0, 2)); outs[5].append(hst)

    y_prompt = xp.reshape(bsz, seq, D_MODEL)
    y_sample = xs.reshape(dseq, dbsz, D_MODEL).transpose(1, 0, 2)
    return (y_prompt, y_sample) + tuple(o[0][None] if depth == 1 else jnp.stack(o) for o in outs)
```

```python
import functools

import jax
import jax.numpy as jnp
from jax import lax
from jax.experimental import pallas as pl
from jax.experimental.pallas import tpu as pltpu

D_MODEL = 1024
D_CONV = 512
D_LRU = 512
D_IN = 2 * D_CONV + 2 * D_LRU
D_FF = 2816
D_PLE = 256
LRU_HEADS = 8
CONV_WIDTH = 31
LRU_CONV_WIDTH = 4
LRU_C = 8.0
EPS = 1e-6
FFN_RES_WEIGHT = 0.5

SUBLANES = 8
LANES = 128
VMEM_LIMIT_BYTES = 56 * 1024 * 1024

TOKEN_TILE = 1024
TIME_TILE = 512
CONV_ROWS = 64
CONV_PAD = 32
LCONV_PAD = 8
SAMPLE_BATCH_TILE = 64
ANCHOR_ROWS = 16

FF_CHUNKS = tuple((lo, min(512, D_FF - lo)) for lo in range(0, D_FF, 512))

FF_COLS = 256
DOWN_COLS = 256
IN_COLS = 512
PART_ROWS = 128
PIECES_PER_FF = ("CCCC", "LC", "LC", "LC", "LC", "LC", "LC", "LC", "LC", "LC", "LC")
PIECES_PER_DOWN = ("", "LLCCC", "LLCCC", "LLCC")
PIECES_PER_IN = ("LLLLCCCC", "LCCCCCC", "LLLLCCCC", "F")

_bf16 = jnp.bfloat16
_f32 = jnp.float32


def _rms(x, g):
    ms = jnp.mean(x * x, axis=-1, keepdims=True)
    return x * lax.rsqrt(ms + EPS) * g


def _dot(a, b):
    return jnp.dot(a, b, preferred_element_type=_f32)


def _ffn(x, pre_g, post_g, wgu_ref, wd_ref, act_ref):
    h = _rms(x, pre_g).astype(_bf16)
    for lo, n in FF_CHUNKS:
        gate = _dot(h, wgu_ref[:, lo:lo + n])
        up = _dot(h, wgu_ref[:, D_FF + lo:D_FF + lo + n])
        act_ref[:, lo:lo + n] = (gate * jax.nn.sigmoid(gate) * up).astype(_bf16)
    y = _dot(act_ref[...], wd_ref[...])
    return x + FFN_RES_WEIGHT * _rms(y, post_g)


def _ffn_ple_kernel(x_ref, p_ref, pre_g, post_g, wgu, wd, ple_g, w_gate, w_proj, ple_post_g,
                    y_ref, act_ref):
    x3 = _ffn(x_ref[...], pre_g[...], post_g[...], wgu, wd, act_ref)
    gate = jax.nn.sigmoid(_dot(_rms(x3, ple_g[...]).astype(_bf16), w_gate[...]))
    e = _dot(p_ref[...].astype(_bf16), w_proj[...])
    y_ref[...] = x3 + _rms(gate * e, ple_post_g[...])


def _const_spec(shape):
    zeros = (0,) * len(shape)
    return pl.BlockSpec(shape, lambda *_: zeros, pipeline_mode=pl.Buffered(1))


def _row_spec(tile, width):
    return pl.BlockSpec((tile, width), lambda i: (i, 0))


def _ffn_ple_call(x, p, pre_g, post_g, wgu, wd, ple_g, w_gate, w_proj, ple_post_g):
    n = x.shape[0]
    tile = min(TOKEN_TILE, n)
    return pl.pallas_call(
        _ffn_ple_kernel,
        grid=(n // tile,),
        in_specs=[_row_spec(tile, D_MODEL), _row_spec(tile, D_PLE),
                  _const_spec(pre_g.shape), _const_spec(post_g.shape),
                  _const_spec(wgu.shape), _const_spec(wd.shape), _const_spec(ple_g.shape),
                  _const_spec(w_gate.shape), _const_spec(w_proj.shape), _const_spec(ple_post_g.shape)],
        out_specs=_row_spec(tile, D_MODEL),
        out_shape=jax.ShapeDtypeStruct((n, D_MODEL), _f32),
        scratch_shapes=[pltpu.VMEM((tile, D_FF), _bf16)],
        compiler_params=pltpu.CompilerParams(dimension_semantics=("arbitrary",),
                                             vmem_limit_bytes=VMEM_LIMIT_BYTES),
        name="ffn2_ple",
    )(x, p, pre_g, post_g, wgu, wd, ple_g, w_gate, w_proj, ple_post_g)


def _lru_gates(xc, w_gates, ba, bx):
    gates = _dot(xc.astype(_bf16), w_gates)
    return jax.nn.sigmoid(gates[:, :D_LRU] + ba), jax.nn.sigmoid(gates[:, D_LRU:] + bx)


def _lru_ab(xc, r, i_g, lam):
    neg_lam = -lam
    softplus = jnp.maximum(neg_lam, 0.0) + jnp.log1p(jnp.exp(-jnp.abs(neg_lam)))
    log_a = -LRU_C * r * softplus
    a = jnp.exp(log_a)
    b = jnp.sqrt(-jnp.tanh(log_a) * (a * a + 1.0)) * (i_g * xc)
    return a, b


def _causal_taps(buf, r0, c0, rows, pad, width, w_ref, b_ref):
    first = pad - (width - 1)
    x = buf[r0:r0 + rows + pad, c0:c0 + LANES]
    acc = jnp.broadcast_to(b_ref[:, c0:c0 + LANES], (rows, LANES))
    for phase in range(SUBLANES):
        taps = [(q, q * SUBLANES + phase - first) for q in range(pad // SUBLANES + 1)]
        taps = [(q, k) for q, k in taps if 0 <= k < width]
        if not taps:
            continue
        xs = x if phase == 0 else pltpu.roll(x, rows + pad - phase, axis=0)
        for q, k in taps:
            acc = acc + w_ref[k:k + 1, c0:c0 + LANES] * xs[q * SUBLANES:q * SUBLANES + rows, :]
    return acc


def _anchored(value, chains, which):
    pieces = [p for c in which for p in next(chains[c])]
    if not pieces:
        return value
    folded = []
    for p in pieces:
        m = jnp.min(p.reshape(-1, SUBLANES, p.shape[-1]), axis=0)
        folded += [m[:, c0:c0 + LANES] for c0 in range(0, m.shape[1], LANES)]
    zero = jnp.minimum(jnp.abs(functools.reduce(jnp.minimum, folded)), 0.0)
    zero = jnp.concatenate([zero] * (ANCHOR_ROWS // SUBLANES), axis=0).astype(value.dtype)
    zero = jnp.concatenate([zero] * (value.shape[1] // LANES), axis=1)
    split = value.shape[0] - ANCHOR_ROWS
    return jnp.concatenate([value[0:split], value[split:] + zero], axis=0)


def _ffn_anchored(x, pre_g, post_g, wgu_ref, wd_ref, act_ref, chains, per_ff, per_down):
    h = _rms(x, pre_g).astype(_bf16)
    for lo, which in zip(range(0, D_FF, FF_COLS), per_ff, strict=True):
        gate = _dot(h, wgu_ref[:, lo:lo + FF_COLS])
        up = _dot(h, wgu_ref[:, D_FF + lo:D_FF + lo + FF_COLS])
        act_ref[:, lo:lo + FF_COLS] = _anchored(gate * jax.nn.sigmoid(gate) * up, chains, which).astype(_bf16)
    ys = [_dot(_anchored(act_ref[...], chains, which), wd_ref[:, lo:lo + DOWN_COLS])
          for lo, which in zip(range(0, D_MODEL, DOWN_COLS), per_down, strict=True)]
    return x + FFN_RES_WEIGHT * _rms(jnp.concatenate(ys, axis=1), post_g)


def _lru_steps(fresh, lconv_w, lconv_b, w_gates, ba, bx, lam, ubuf, lconv_st_ref, h_st_ref,
               xbuf, xcbuf, hbuf, ybuf, hcar):
    tt = ubuf.shape[0]
    parts = [(lo, lo + PART_ROWS) for lo in range(0, tt, PART_ROWS)]

    for lo, hi in parts:
        blocks = []
        for r0 in range(lo, hi, CONV_ROWS):
            for c0 in range(0, D_LRU, LANES):
                xc = _causal_taps(xbuf, r0, c0, CONV_ROWS, LCONV_PAD, LRU_CONV_WIDTH, lconv_w, lconv_b)
                xcbuf[r0:r0 + CONV_ROWS, c0:c0 + LANES] = xc
                blocks.append(xc)
        yield blocks
    lconv_st_ref[0] = xbuf[pl.ds(LCONV_PAD + tt - (LRU_CONV_WIDTH - 1), LRU_CONV_WIDTH - 1), :]

    gates = []
    for lo, hi in parts:
        xc = xcbuf[lo:hi, :]
        r, i_g = _lru_gates(xc, w_gates[...], ba[...], bx[...])
        gates.append((xc, r, i_g))
        yield [r, i_g]
    coeffs = []
    for xc, r, i_g in gates:
        a, b = _lru_ab(xc, r, i_g, lam[...])
        coeffs.append((a, b))
        yield [b]

    scanned = []
    for a, b in coeffs:
        for lo in range(0, PART_ROWS, CONV_ROWS):
            aq = a[lo:lo + CONV_ROWS].reshape(CONV_ROWS // SUBLANES, SUBLANES, D_LRU)
            bq = b[lo:lo + CONV_ROWS].reshape(aq.shape)
            row = lax.broadcasted_iota(jnp.int32, aq.shape, 1)
            shift = 1
            while shift < SUBLANES:
                keep = row >= shift
                bq = jnp.where(keep, aq * pltpu.roll(bq, shift, axis=1) + bq, bq)
                aq = jnp.where(keep, aq * pltpu.roll(aq, shift, axis=1), aq)
                shift *= 2
            scanned.append((aq, bq))
            yield [aq, bq]

    carry = jnp.where(fresh, 0.0, hcar[...])
    r0 = 0
    for aq, bq in scanned:
        for i in range(aq.shape[0]):
            rows = aq[i] * carry + bq[i]
            hbuf[r0:r0 + SUBLANES, :] = rows
            carry = rows[SUBLANES - 1:SUBLANES, :]
            r0 += SUBLANES
    hcar[...] = carry
    h_st_ref[0] = carry
    yield [rows]

    for lo, hi in parts:
        yb = hbuf[lo:hi, :] * jax.nn.gelu(ubuf[lo:hi, 2 * D_CONV + D_LRU:D_IN])
        ybuf[lo:hi, :] = yb.astype(_bf16)
        yield [yb]


def _conv_steps(conv_w, conv_b, conv_g, ubuf, conv_st_ref, gbuf, cbuf):
    tt = cbuf.shape[0]
    for lo in range(0, tt, PART_ROWS):
        g = ubuf[lo:lo + PART_ROWS, 0:D_CONV] * jax.nn.sigmoid(ubuf[lo:lo + PART_ROWS, D_CONV:2 * D_CONV])
        gbuf[CONV_PAD + lo:CONV_PAD + lo + PART_ROWS, :] = g
        yield [g]
    conv_st_ref[0] = gbuf[pl.ds(CONV_PAD + tt - (CONV_WIDTH - 1), CONV_WIDTH - 1), :]
    for r0 in range(0, tt, CONV_ROWS):
        cols = []
        for c0 in range(0, D_CONV, LANES):
            cols.append(_causal_taps(gbuf, r0, c0, CONV_ROWS, CONV_PAD, CONV_WIDTH, conv_w, conv_b))
            if len(cols) * LANES < D_CONV:
                yield [cols[-1]]
        cn = _rms(jnp.concatenate(cols, axis=1), conv_g[...])
        c = cn * jax.nn.sigmoid(cn)
        cbuf[r0:r0 + CONV_ROWS, :] = c.astype(_bf16)
        yield [c]


def _out_steps(w_out, post_g, x2_ref, cbuf, ybuf):
    out = _dot(cbuf[...], w_out[0:D_CONV, :]) + _dot(ybuf[...], w_out[D_CONV:D_CONV + D_LRU, :])
    delta = _rms(out, post_g[...])
    x2_ref[...] += delta
    yield [delta]


def _prompt_a_kernel(x_ref, xs_ref, pre_g, post_g, wgu, wd, mix_g, w_in,
                     conv_w, conv_b, conv_g, lconv_w, lconv_b, w_gates, ba, bx, lam, w_out, mix_post_g,
                     x2_ref, conv_st_ref, lconv_st_ref, h_st_ref, x1s_hbm, us_hbm,
                     act_ref, ubuf, x1buf, gbuf, xbuf, cbuf, xcbuf, hbuf, ybuf, hcar, sems, *, tiles_per_seq):
    s = pl.program_id(0)
    last = pl.num_programs(0) - 1
    tt = x_ref.shape[0]

    @pl.when(s == 0)
    def _():
        ubuf[...] = jnp.zeros(ubuf.shape, _f32)
        x1buf[...] = jnp.zeros(x1buf.shape, _f32)
        gbuf[...] = jnp.zeros(gbuf.shape, _f32)
        xbuf[...] = jnp.zeros(xbuf.shape, _f32)
        hcar[...] = jnp.zeros(hcar.shape, _f32)

    fresh = lax.rem(s + tiles_per_seq - 1, tiles_per_seq) == 0
    x2_ref[...] = x1buf[...]
    gbuf[0:CONV_PAD, :] = jnp.where(fresh, 0.0, gbuf[tt:tt + CONV_PAD, :])
    xbuf[0:LCONV_PAD, :] = jnp.where(fresh, 0.0, xbuf[tt:tt + LCONV_PAD, :])
    xbuf[LCONV_PAD:LCONV_PAD + tt, :] = ubuf[:, 2 * D_CONV:2 * D_CONV + D_LRU]
    chains = {
        "L": _lru_steps(fresh, lconv_w, lconv_b, w_gates, ba, bx, lam, ubuf, lconv_st_ref, h_st_ref,
                        xbuf, xcbuf, hbuf, ybuf, hcar),
        "C": _conv_steps(conv_w, conv_b, conv_g, ubuf, conv_st_ref, gbuf, cbuf),
        "F": _out_steps(w_out, mix_post_g, x2_ref, cbuf, ybuf),
    }

    x = jnp.where(s == last, xs_ref[...], x_ref[...])
    x1 = _ffn_anchored(x, pre_g[...], post_g[...], wgu, wd, act_ref, chains, PIECES_PER_FF, PIECES_PER_DOWN)
    x1buf[...] = x1
    hm = _rms(x1, mix_g[...]).astype(_bf16)
    for lo, which in zip(range(0, D_IN, IN_COLS), PIECES_PER_IN, strict=True):
        ubuf[:, lo:lo + IN_COLS] = _anchored(_dot(hm, w_in[:, lo:lo + IN_COLS]), chains, which)
    assert all(next(chain, None) is None for chain in chains.values())

    @pl.when(s == last)
    def _():
        copies = [pltpu.make_async_copy(x1buf, x1s_hbm, sems.at[0]),
                  pltpu.make_async_copy(ubuf, us_hbm, sems.at[1])]
        for copy in copies:
            copy.start()
        for copy in copies:
            copy.wait()


def _prompt_a_call(x, xs, bsz, pre_g, post_g, wgu, wd, mix_g, w_in, conv_w, conv_b, conv_g, lconv_w, lconv_b,
                   w_gates, ba, bx, lam, w_out, mix_post_g):
    n = x.shape[0]
    tt = TIME_TILE
    tiles = n // tt
    tiles_per_seq = tiles // bsz
    assert xs.shape == (tt, D_MODEL), "the sample group must fill exactly one tile of the first chain"
    consts = (pre_g, post_g, wgu, wd, mix_g, w_in, conv_w, conv_b, conv_g, lconv_w, lconv_b, w_gates, ba, bx,
              lam, w_out, mix_post_g)
    prev_seq = lambda s: (jnp.maximum(s - 1, 0) // tiles_per_seq, 0, 0)
    return pl.pallas_call(
        functools.partial(_prompt_a_kernel, tiles_per_seq=tiles_per_seq),
        grid=(tiles + 1,),
        in_specs=[pl.BlockSpec((tt, D_MODEL), lambda s: (jnp.minimum(s, tiles - 1), 0)), _const_spec(xs.shape)]
                 + [_const_spec(c.shape) for c in consts],
        out_specs=[pl.BlockSpec((tt, D_MODEL), lambda s: (jnp.maximum(s - 1, 0), 0)),
                   pl.BlockSpec((1, CONV_WIDTH - 1, D_CONV), prev_seq),
                   pl.BlockSpec((1, LRU_CONV_WIDTH - 1, D_LRU), prev_seq),
                   pl.BlockSpec((1, 1, D_LRU), prev_seq),
                   pl.BlockSpec(memory_space=pl.ANY),
                   pl.BlockSpec(memory_space=pl.ANY)],
        out_shape=[jax.ShapeDtypeStruct((n, D_MODEL), _f32),
                   jax.ShapeDtypeStruct((bsz, CONV_WIDTH - 1, D_CONV), _f32),
                   jax.ShapeDtypeStruct((bsz, LRU_CONV_WIDTH - 1, D_LRU), _f32),
                   jax.ShapeDtypeStruct((bsz, 1, D_LRU), _f32),
                   jax.ShapeDtypeStruct((tt, D_MODEL), _f32),
                   jax.ShapeDtypeStruct((tt, D_IN), _f32)],
        scratch_shapes=[pltpu.VMEM((tt, D_FF), _bf16),
                        pltpu.VMEM((tt, D_IN), _f32),
                        pltpu.VMEM((tt, D_MODEL), _f32),
                        pltpu.VMEM((CONV_PAD + tt, D_CONV), _f32),
                        pltpu.VMEM((LCONV_PAD + tt, D_LRU), _f32),
                        pltpu.VMEM((tt, D_CONV), _bf16),
                        pltpu.VMEM((tt, D_LRU), _f32),
                        pltpu.VMEM((tt, D_LRU), _f32),
                        pltpu.VMEM((tt, D_LRU), _bf16),
                        pltpu.VMEM((1, D_LRU), _f32),
                        pltpu.SemaphoreType.DMA((2,))],
        compiler_params=pltpu.CompilerParams(dimension_semantics=("arbitrary",),
                                             vmem_limit_bytes=VMEM_LIMIT_BYTES),
        name="prompt_ffn1_mix",
    )(x, xs, *consts)


def _mix_sample_kernel(u_ref, x1_ref, cache_ref, lstate_ref, h0_ref,
                       conv_w, conv_b, conv_g, lconv_w, lconv_b, w_gates, ba, bx, lam, w_out, post_g,
                       x2_ref, conv_st_ref, lconv_st_ref, h_st_ref):
    steps = u_ref.shape[0]
    past = CONV_WIDTH - 1
    lpast = LRU_CONV_WIDTH - 1

    gp = [cache_ref[j] for j in range(past)]
    gp += [u_ref[t, :, 0:D_CONV] * jax.nn.sigmoid(u_ref[t, :, D_CONV:2 * D_CONV]) for t in range(steps)]
    for j in range(past):
        conv_st_ref[j] = gp[j + steps]
    c_rows = []
    for t in range(steps):
        acc = jnp.broadcast_to(conv_b[...], gp[0].shape)
        for k in range(CONV_WIDTH):
            acc = acc + conv_w[k:k + 1, :] * gp[t + k]
        cn = _rms(acc, conv_g[...])
        c_rows.append(cn * jax.nn.sigmoid(cn))
    c = jnp.concatenate(c_rows, axis=0).astype(_bf16)

    xp = [lstate_ref[j] for j in range(lpast)]
    xp += [u_ref[t, :, 2 * D_CONV:2 * D_CONV + D_LRU] for t in range(steps)]
    for j in range(lpast):
        lconv_st_ref[j] = xp[j + steps]
    xc_rows = []
    for t in range(steps):
        acc = jnp.broadcast_to(lconv_b[...], xp[0].shape)
        for k in range(LRU_CONV_WIDTH):
            acc = acc + lconv_w[k:k + 1, :] * xp[t + k]
        xc_rows.append(acc)
    xc = jnp.concatenate(xc_rows, axis=0)
    r, i_g = _lru_gates(xc, w_gates[...], ba[...], bx[...])
    a, b = _lru_ab(xc, r, i_g, lam[...])
    bt = h0_ref.shape[0]
    h = h0_ref[...]
    hs = []
    for t in range(steps):
        h = a[t * bt:(t + 1) * bt, :] * h + b[t * bt:(t + 1) * bt, :]
        hs.append(h)
    h_st_ref[...] = h
    u_gelu = jnp.concatenate([u_ref[t, :, 2 * D_CONV + D_LRU:D_IN] for t in range(steps)], axis=0)
    yb = (jnp.concatenate(hs, axis=0) * jax.nn.gelu(u_gelu)).astype(_bf16)

    out = _dot(c, w_out[0:D_CONV, :]) + _dot(yb, w_out[D_CONV:D_CONV + D_LRU, :])
    x2 = _rms(out, post_g[...])
    for t in range(steps):
        x2_ref[t] = x1_ref[t] + x2[t * bt:(t + 1) * bt, :]


def _mix_sample_call(u, x1, cache, lstate, h0, conv_w, conv_b, conv_g, lconv_w, lconv_b, w_gates, ba, bx,
                     lam, w_out, post_g):
    steps, bsz, _ = u.shape
    bt = SAMPLE_BATCH_TILE
    consts = (conv_w, conv_b, conv_g, lconv_w, lconv_b, w_gates, ba, bx, lam, w_out, post_g)
    return pl.pallas_call(
        _mix_sample_kernel,
        grid=(bsz // bt,),
        in_specs=[pl.BlockSpec((steps, bt, D_IN), lambda i: (0, i, 0)),
                  pl.BlockSpec((steps, bt, D_MODEL), lambda i: (0, i, 0)),
                  pl.BlockSpec((CONV_WIDTH - 1, bt, D_CONV), lambda i: (0, i, 0)),
                  pl.BlockSpec((LRU_CONV_WIDTH - 1, bt, D_LRU), lambda i: (0, i, 0)),
                  pl.BlockSpec((bt, D_LRU), lambda i: (i, 0))]
                 + [_const_spec(c.shape) for c in consts],
        out_specs=[pl.BlockSpec((steps, bt, D_MODEL), lambda i: (0, i, 0)),
                   pl.BlockSpec((CONV_WIDTH - 1, bt, D_CONV), lambda i: (0, i, 0)),
                   pl.BlockSpec((LRU_CONV_WIDTH - 1, bt, D_LRU), lambda i: (0, i, 0)),
                   pl.BlockSpec((bt, D_LRU), lambda i: (i, 0))],
        out_shape=[jax.ShapeDtypeStruct((steps, bsz, D_MODEL), _f32),
                   jax.ShapeDtypeStruct((CONV_WIDTH - 1, bsz, D_CONV), _f32),
                   jax.ShapeDtypeStruct((LRU_CONV_WIDTH - 1, bsz, D_LRU), _f32),
                   jax.ShapeDtypeStruct((bsz, D_LRU), _f32)],
        compiler_params=pltpu.CompilerParams(dimension_semantics=("arbitrary",),
                                             vmem_limit_bytes=VMEM_LIMIT_BYTES),
        name="mix_sample",
    )(u, x1, cache, lstate, h0, *consts)


def _block_diag(w):
    heads, d, _ = w.shape
    tiled = jnp.tile(w.reshape(heads * d, d), (1, heads))
    rows = lax.broadcasted_iota(jnp.int32, tiled.shape, 0) // d
    cols = lax.broadcasted_iota(jnp.int32, tiled.shape, 1) // d
    return jnp.where(rows == cols, tiled, 0.0)


def kernel(x_prompt, x_sample, cache_conv, state_lru_conv, state_lru_h, p_prompt, p_sample, ffn1_pre_g, ffn1_post_g, ffn1_w_gu, ffn1_w_down, mix_pre_g, mix_post_g, w_in, conv_w, conv_b, conv_norm_g, lru_conv_w, lru_conv_b, lru_wa, lru_ba, lru_wx, lru_bx, lru_lambda, w_out, ffn2_pre_g, ffn2_post_g, ffn2_w_gu, ffn2_w_down, ple_norm_g, ple_w_gate, ple_w_proj, ple_post_g):
    depth = ffn1_w_gu.shape[0]
    bsz, seq, _ = x_prompt.shape
    dbsz, dseq, _ = x_sample.shape

    xp = x_prompt.reshape(bsz * seq, D_MODEL)
    xs = x_sample.transpose(1, 0, 2).reshape(dseq * dbsz, D_MODEL)
    outs = [[] for _ in range(6)]
    for l in range(depth):
        row = lambda v: v[l:l + 1]
        wgu1, wd1 = ffn1_w_gu[l].astype(_bf16), ffn1_w_down[l].astype(_bf16)
        wgu2, wd2 = ffn2_w_gu[l].astype(_bf16), ffn2_w_down[l].astype(_bf16)
        w_in_l, w_out_l = w_in[l].astype(_bf16), w_out[l].astype(_bf16)
        w_gate_l, w_proj_l = ple_w_gate[l].astype(_bf16), ple_w_proj[l].astype(_bf16)
        w_gates = jnp.concatenate([_block_diag(lru_wa[l]), _block_diag(lru_wx[l])], axis=1).astype(_bf16)
        conv_consts = (conv_w[l], row(conv_b), row(conv_norm_g))
        lru_consts = (lru_conv_w[l], row(lru_conv_b), w_gates, row(lru_ba), row(lru_bx), row(lru_lambda))
        ffn1_args = (row(ffn1_pre_g), row(ffn1_post_g), wgu1, wd1, row(mix_pre_g), w_in_l)
        ffn2_args = (row(ffn2_pre_g), row(ffn2_post_g), wgu2, wd2, row(ple_norm_g), w_gate_l, w_proj_l,
                     row(ple_post_g))

        x2, cst, lst, hst, x1, u = _prompt_a_call(xp, xs, bsz, *ffn1_args, *conv_consts, *lru_consts, w_out_l,
                                                  row(mix_post_g))
        xp = _ffn_ple_call(x2, p_prompt[l].reshape(bsz * seq, D_PLE), *ffn2_args)
        outs[0].append(cst); outs[1].append(lst); outs[2].append(hst.reshape(bsz, D_LRU))

        x2, cst, lst, hst = _mix_sample_call(u.reshape(dseq, dbsz, D_IN), x1.reshape(dseq, dbsz, D_MODEL),
                                             cache_conv[l].transpose(1, 0, 2), state_lru_conv[l].transpose(1, 0, 2),
                                             state_lru_h[l], *conv_consts,
                                             *lru_consts, w_out_l, row(mix_post_g))
        ps = p_sample[l].transpose(1, 0, 2).reshape(dseq * dbsz, D_PLE)
        xs = _ffn_ple_call(x2.reshape(dseq * dbsz, D_MODEL), ps, *ffn2_args)
        outs[3].append(cst.transpose(1, 0, 2)); outs[4].append(lst.transpose(1, 0, 2)); outs[5].append(hst)

    y_prompt = xp.reshape(bsz, seq, D_MODEL)
    y_sample = xs.reshape(dseq, dbsz, D_MODEL).transpose(1, 0, 2)
    return (y_prompt, y_sample) + tuple(o[0][None] if depth == 1 else jnp.stack(o) for o in outs)
```

```python
import functools

import jax
import jax.numpy as jnp
from jax import lax
from jax.experimental import pallas as pl
from jax.experimental.pallas import tpu as pltpu

D_MODEL = 1024
D_CONV = 512
D_LRU = 512
D_IN = 2 * D_CONV + 2 * D_LRU
D_FF = 2816
D_PLE = 256
LRU_HEADS = 8
CONV_WIDTH = 31
LRU_CONV_WIDTH = 4
LRU_C = 8.0
EPS = 1e-6
FFN_RES_WEIGHT = 0.5

SUBLANES = 8
LANES = 128
VMEM_LIMIT_BYTES = 56 * 1024 * 1024

TOKEN_TILE = 1024
TIME_TILE = 512
CONV_ROWS = 64
CONV_PAD = 32
LCONV_PAD = 8
SAMPLE_BATCH_TILE = 64
ANCHOR_ROWS = 16

FF_CHUNKS = tuple((lo, min(512, D_FF - lo)) for lo in range(0, D_FF, 512))

FF_COLS = 256
DOWN_COLS = 256
IN_COLS = 512
PART_ROWS = 128
PIECES_PER_FF = ("CCCC", "LC", "LC", "LC", "LC", "LC", "LC", "LC", "LC", "LC", "LC")
PIECES_PER_DOWN = ("", "LLCCC", "LLCCC", "LLCC")
PIECES_PER_IN = ("LLLLCCCC", "LCCCCCC", "LLLLCCCC", "F")

_bf16 = jnp.bfloat16
_f32 = jnp.float32


def _rms(x, g):
    ms = jnp.mean(x * x, axis=-1, keepdims=True)
    return x * lax.rsqrt(ms + EPS) * g


def _dot(a, b):
    return jnp.dot(a, b, preferred_element_type=_f32)


def _ffn_ple_stages(x_ref, p_ref, pre_g, post_g, wgu_ref, wd_ref, ple_g, w_gate, w_proj, ple_post_g, y_ref,
                    act_ref):
    x = x_ref[...]
    h = _rms(x, pre_g[...]).astype(_bf16)
    yield
    for lo, n in FF_CHUNKS:
        gate = _dot(h, wgu_ref[:, lo:lo + n])
        up = _dot(h, wgu_ref[:, D_FF + lo:D_FF + lo + n])
        act_ref[:, lo:lo + n] = (gate * jax.nn.sigmoid(gate) * up).astype(_bf16)
    yield
    x3 = x + FFN_RES_WEIGHT * _rms(_dot(act_ref[...], wd_ref[...]), post_g[...])
    yield
    gate = jax.nn.sigmoid(_dot(_rms(x3, ple_g[...]).astype(_bf16), w_gate[...]))
    e = _dot(p_ref[...].astype(_bf16), w_proj[...])
    yield
    y_ref[...] = x3 + _rms(gate * e, ple_post_g[...])
    yield


HALF_STAGE_ORDER = "AABABABABB"


def _ffn_ple_kernel(x_ref, p_ref, pre_g, post_g, wgu, wd, ple_g, w_gate, w_proj, ple_post_g,
                    y_ref, act_ref):
    half = x_ref.shape[0] // 2
    stages = {name: _ffn_ple_stages(x_ref.at[lo:lo + half], p_ref.at[lo:lo + half], pre_g, post_g, wgu, wd, ple_g,
                                    w_gate, w_proj, ple_post_g, y_ref.at[lo:lo + half], act_ref.at[lo:lo + half])
              for name, lo in (("A", 0), ("B", half))}
    for name in HALF_STAGE_ORDER:
        next(stages[name])


def _const_spec(shape):
    zeros = (0,) * len(shape)
    return pl.BlockSpec(shape, lambda *_: zeros, pipeline_mode=pl.Buffered(1))


def _row_spec(tile, width):
    return pl.BlockSpec((tile, width), lambda i: (i, 0))


def _ffn_ple_call(x, p, pre_g, post_g, wgu, wd, ple_g, w_gate, w_proj, ple_post_g):
    n = x.shape[0]
    tile = min(TOKEN_TILE, n)
    return pl.pallas_call(
        _ffn_ple_kernel,
        grid=(n // tile,),
        in_specs=[_row_spec(tile, D_MODEL), _row_spec(tile, D_PLE),
                  _const_spec(pre_g.shape), _const_spec(post_g.shape),
                  _const_spec(wgu.shape), _const_spec(wd.shape), _const_spec(ple_g.shape),
                  _const_spec(w_gate.shape), _const_spec(w_proj.shape), _const_spec(ple_post_g.shape)],
        out_specs=_row_spec(tile, D_MODEL),
        out_shape=jax.ShapeDtypeStruct((n, D_MODEL), _f32),
        scratch_shapes=[pltpu.VMEM((tile, D_FF), _bf16)],
        compiler_params=pltpu.CompilerParams(dimension_semantics=("arbitrary",),
                                             vmem_limit_bytes=VMEM_LIMIT_BYTES),
        name="ffn2_ple",
    )(x, p, pre_g, post_g, wgu, wd, ple_g, w_gate, w_proj, ple_post_g)


def _lru_gates(xc, w_gates, ba, bx):
    gates = _dot(xc.astype(_bf16), w_gates)
    return jax.nn.sigmoid(gates[:, :D_LRU] + ba), jax.nn.sigmoid(gates[:, D_LRU:] + bx)


def _lru_ab(xc, r, i_g, lam):
    neg_lam = -lam
    softplus = jnp.maximum(neg_lam, 0.0) + jnp.log1p(jnp.exp(-jnp.abs(neg_lam)))
    log_a = -LRU_C * r * softplus
    a = jnp.exp(log_a)
    b = jnp.sqrt(-jnp.tanh(log_a) * (a * a + 1.0)) * (i_g * xc)
    return a, b


def _causal_taps(buf, r0, c0, rows, pad, width, w_ref, b_ref):
    first = pad - (width - 1)
    x = buf[r0:r0 + rows + pad, c0:c0 + LANES]
    acc = jnp.broadcast_to(b_ref[:, c0:c0 + LANES], (rows, LANES))
    for phase in range(SUBLANES):
        taps = [(q, q * SUBLANES + phase - first) for q in range(pad // SUBLANES + 1)]
        taps = [(q, k) for q, k in taps if 0 <= k < width]
        if not taps:
            continue
        xs = x if phase == 0 else pltpu.roll(x, rows + pad - phase, axis=0)
        for q, k in taps:
            acc = acc + w_ref[k:k + 1, c0:c0 + LANES] * xs[q * SUBLANES:q * SUBLANES + rows, :]
    return acc


def _anchored(value, chains, which):
    pieces = [p for c in which for p in next(chains[c])]
    if callable(value):
        value = value()
    if not pieces:
        return value
    folded = []
    for p in pieces:
        m = jnp.min(p.reshape(-1, SUBLANES, p.shape[-1]), axis=0)
        folded += [m[:, c0:c0 + LANES] for c0 in range(0, m.shape[1], LANES)]
    zero = jnp.minimum(jnp.abs(functools.reduce(jnp.minimum, folded)), 0.0)
    zero = jnp.concatenate([zero] * (ANCHOR_ROWS // SUBLANES), axis=0).astype(value.dtype)
    zero = jnp.concatenate([zero] * (value.shape[1] // LANES), axis=1)
    split = value.shape[0] - ANCHOR_ROWS
    return jnp.concatenate([value[0:split], value[split:] + zero], axis=0)


def _ffn_anchored(x, pre_g, post_g, wgu_ref, wd_ref, act_ref, chains, per_ff, per_down):
    h = _rms(x, pre_g).astype(_bf16)
    for lo, which in zip(range(0, D_FF, FF_COLS), per_ff, strict=True):
        gate = _dot(h, wgu_ref[:, lo:lo + FF_COLS])
        up = _dot(h, wgu_ref[:, D_FF + lo:D_FF + lo + FF_COLS])
        act = _anchored(lambda: gate * jax.nn.sigmoid(gate) * up, chains, which)
        act_ref[:, lo:lo + FF_COLS] = act.astype(_bf16)
    ys = [_dot(_anchored(act_ref[...], chains, which), wd_ref[:, lo:lo + DOWN_COLS])
          for lo, which in zip(range(0, D_MODEL, DOWN_COLS), per_down, strict=True)]
    return x + FFN_RES_WEIGHT * _rms(jnp.concatenate(ys, axis=1), post_g)


def _lru_steps(fresh, lconv_w, lconv_b, w_gates, ba, bx, lam, ubuf, lconv_st_ref, h_st_ref,
               xbuf, xcbuf, hbuf, ybuf, hcar):
    tt = ubuf.shape[0]
    parts = [(lo, lo + PART_ROWS) for lo in range(0, tt, PART_ROWS)]

    for lo, hi in parts:
        blocks = []
        for r0 in range(lo, hi, CONV_ROWS):
            for c0 in range(0, D_LRU, LANES):
                xc = _causal_taps(xbuf, r0, c0, CONV_ROWS, LCONV_PAD, LRU_CONV_WIDTH, lconv_w, lconv_b)
                xcbuf[r0:r0 + CONV_ROWS, c0:c0 + LANES] = xc
                blocks.append(xc)
        yield blocks
    lconv_st_ref[0] = xbuf[pl.ds(LCONV_PAD + tt - (LRU_CONV_WIDTH - 1), LRU_CONV_WIDTH - 1), :]

    gates = []
    for lo, hi in parts:
        xc = xcbuf[lo:hi, :]
        r, i_g = _lru_gates(xc, w_gates[...], ba[...], bx[...])
        gates.append((xc, r, i_g))
        yield [r, i_g]
    coeffs = []
    for xc, r, i_g in gates:
        a, b = _lru_ab(xc, r, i_g, lam[...])
        coeffs.append((a, b))
        yield [b]

    scanned = []
    for a, b in coeffs:
        for lo in range(0, PART_ROWS, CONV_ROWS):
            aq = a[lo:lo + CONV_ROWS].reshape(CONV_ROWS // SUBLANES, SUBLANES, D_LRU)
            bq = b[lo:lo + CONV_ROWS].reshape(aq.shape)
            row = lax.broadcasted_iota(jnp.int32, aq.shape, 1)
            shift = 1
            while shift < SUBLANES:
                keep = row >= shift
                bq = jnp.where(keep, aq * pltpu.roll(bq, shift, axis=1) + bq, bq)
                aq = jnp.where(keep, aq * pltpu.roll(aq, shift, axis=1), aq)
                shift *= 2
            scanned.append((aq, bq))
            yield [aq, bq]

    carry = jnp.where(fresh, 0.0, hcar[...])
    r0 = 0
    for aq, bq in scanned:
        for i in range(aq.shape[0]):
            rows = aq[i] * carry + bq[i]
            hbuf[r0:r0 + SUBLANES, :] = rows
            carry = rows[SUBLANES - 1:SUBLANES, :]
            r0 += SUBLANES
    hcar[...] = carry
    h_st_ref[0] = carry
    yield [rows]

    for lo, hi in parts:
        yb = hbuf[lo:hi, :] * jax.nn.gelu(ubuf[lo:hi, 2 * D_CONV + D_LRU:D_IN])
        ybuf[lo:hi, :] = yb.astype(_bf16)
        yield [yb]


def _conv_steps(conv_w, conv_b, conv_g, ubuf, conv_st_ref, gbuf, cbuf):
    tt = cbuf.shape[0]
    for lo in range(0, tt, PART_ROWS):
        g = ubuf[lo:lo + PART_ROWS, 0:D_CONV] * jax.nn.sigmoid(ubuf[lo:lo + PART_ROWS, D_CONV:2 * D_CONV])
        gbuf[CONV_PAD + lo:CONV_PAD + lo + PART_ROWS, :] = g
        yield [g]
    conv_st_ref[0] = gbuf[pl.ds(CONV_PAD + tt - (CONV_WIDTH - 1), CONV_WIDTH - 1), :]
    for r0 in range(0, tt, CONV_ROWS):
        cols = []
        for c0 in range(0, D_CONV, LANES):
            cols.append(_causal_taps(gbuf, r0, c0, CONV_ROWS, CONV_PAD, CONV_WIDTH, conv_w, conv_b))
            if len(cols) * LANES < D_CONV:
                yield [cols[-1]]
        cn = _rms(jnp.concatenate(cols, axis=1), conv_g[...])
        c = cn * jax.nn.sigmoid(cn)
        cbuf[r0:r0 + CONV_ROWS, :] = c.astype(_bf16)
        yield [c]


def _out_steps(w_out, post_g, x2_ref, cbuf, ybuf):
    out = _dot(cbuf[...], w_out[0:D_CONV, :]) + _dot(ybuf[...], w_out[D_CONV:D_CONV + D_LRU, :])
    delta = _rms(out, post_g[...])
    x2_ref[...] += delta
    yield [delta]


def _prompt_a_kernel(x_ref, xs_ref, pre_g, post_g, wgu, wd, mix_g, w_in,
                     conv_w, conv_b, conv_g, lconv_w, lconv_b, w_gates, ba, bx, lam, w_out, mix_post_g,
                     x2_ref, conv_st_ref, lconv_st_ref, h_st_ref, x1s_hbm, us_hbm,
                     act_ref, ubuf, x1buf, gbuf, xbuf, cbuf, xcbuf, hbuf, ybuf, hcar, sems, *, tiles_per_seq):
    s = pl.program_id(0)
    last = pl.num_programs(0) - 1
    tt = x_ref.shape[0]

    @pl.when(s == 0)
    def _():
        ubuf[...] = jnp.zeros(ubuf.shape, _f32)
        x1buf[...] = jnp.zeros(x1buf.shape, _f32)
        gbuf[...] = jnp.zeros(gbuf.shape, _f32)
        xbuf[...] = jnp.zeros(xbuf.shape, _f32)
        hcar[...] = jnp.zeros(hcar.shape, _f32)

    fresh = lax.rem(s + tiles_per_seq - 1, tiles_per_seq) == 0
    x2_ref[...] = x1buf[...]
    gbuf[0:CONV_PAD, :] = jnp.where(fresh, 0.0, gbuf[tt:tt + CONV_PAD, :])
    xbuf[0:LCONV_PAD, :] = jnp.where(fresh, 0.0, xbuf[tt:tt + LCONV_PAD, :])
    xbuf[LCONV_PAD:LCONV_PAD + tt, :] = ubuf[:, 2 * D_CONV:2 * D_CONV + D_LRU]
    chains = {
        "L": _lru_steps(fresh, lconv_w, lconv_b, w_gates, ba, bx, lam, ubuf, lconv_st_ref, h_st_ref,
                        xbuf, xcbuf, hbuf, ybuf, hcar),
        "C": _conv_steps(conv_w, conv_b, conv_g, ubuf, conv_st_ref, gbuf, cbuf),
        "F": _out_steps(w_out, mix_post_g, x2_ref, cbuf, ybuf),
    }

    x = jnp.where(s == last, xs_ref[...], x_ref[...])
    x1 = _ffn_anchored(x, pre_g[...], post_g[...], wgu, wd, act_ref, chains, PIECES_PER_FF, PIECES_PER_DOWN)
    x1buf[...] = x1
    hm = _rms(x1, mix_g[...]).astype(_bf16)
    for lo, which in zip(range(0, D_IN, IN_COLS), PIECES_PER_IN, strict=True):
        ubuf[:, lo:lo + IN_COLS] = _anchored(_dot(hm, w_in[:, lo:lo + IN_COLS]), chains, which)
    assert all(next(chain, None) is None for chain in chains.values())

    @pl.when(s == last)
    def _():
        copies = [pltpu.make_async_copy(x1buf, x1s_hbm, sems.at[0]),
                  pltpu.make_async_copy(ubuf, us_hbm, sems.at[1])]
        for copy in copies:
            copy.start()
        for copy in copies:
            copy.wait()


def _prompt_a_call(x, xs, bsz, pre_g, post_g, wgu, wd, mix_g, w_in, conv_w, conv_b, conv_g, lconv_w, lconv_b,
                   w_gates, ba, bx, lam, w_out, mix_post_g):
    n = x.shape[0]
    tt = TIME_TILE
    tiles = n // tt
    tiles_per_seq = tiles // bsz
    assert xs.shape == (tt, D_MODEL), "the sample group must fill exactly one tile of the first chain"
    consts = (pre_g, post_g, wgu, wd, mix_g, w_in, conv_w, conv_b, conv_g, lconv_w, lconv_b, w_gates, ba, bx,
              lam, w_out, mix_post_g)
    prev_seq = lambda s: (jnp.maximum(s - 1, 0) // tiles_per_seq, 0, 0)
    return pl.pallas_call(
        functools.partial(_prompt_a_kernel, tiles_per_seq=tiles_per_seq),
        grid=(tiles + 1,),
        in_specs=[pl.BlockSpec((tt, D_MODEL), lambda s: (jnp.minimum(s, tiles - 1), 0)), _const_spec(xs.shape)]
                 + [_const_spec(c.shape) for c in consts],
        out_specs=[pl.BlockSpec((tt, D_MODEL), lambda s: (jnp.maximum(s - 1, 0), 0)),
                   pl.BlockSpec((1, CONV_WIDTH - 1, D_CONV), prev_seq),
                   pl.BlockSpec((1, LRU_CONV_WIDTH - 1, D_LRU), prev_seq),
                   pl.BlockSpec((1, 1, D_LRU), prev_seq),
                   pl.BlockSpec(memory_space=pl.ANY),
                   pl.BlockSpec(memory_space=pl.ANY)],
        out_shape=[jax.ShapeDtypeStruct((n, D_MODEL), _f32),
                   jax.ShapeDtypeStruct((bsz, CONV_WIDTH - 1, D_CONV), _f32),
                   jax.ShapeDtypeStruct((bsz, LRU_CONV_WIDTH - 1, D_LRU), _f32),
                   jax.ShapeDtypeStruct((bsz, 1, D_LRU), _f32),
                   jax.ShapeDtypeStruct((tt, D_MODEL), _f32),
                   jax.ShapeDtypeStruct((tt, D_IN), _f32)],
        scratch_shapes=[pltpu.VMEM((tt, D_FF), _bf16),
                        pltpu.VMEM((tt, D_IN), _f32),
                        pltpu.VMEM((tt, D_MODEL), _f32),
                        pltpu.VMEM((CONV_PAD + tt, D_CONV), _f32),
                        pltpu.VMEM((LCONV_PAD + tt, D_LRU), _f32),
                        pltpu.VMEM((tt, D_CONV), _bf16),
                        pltpu.VMEM((tt, D_LRU), _f32),
                        pltpu.VMEM((tt, D_LRU), _f32),
                        pltpu.VMEM((tt, D_LRU), _bf16),
                        pltpu.VMEM((1, D_LRU), _f32),
                        pltpu.SemaphoreType.DMA((2,))],
        compiler_params=pltpu.CompilerParams(dimension_semantics=("arbitrary",),
                                             vmem_limit_bytes=VMEM_LIMIT_BYTES),
        name="prompt_ffn1_mix",
    )(x, xs, *consts)


def _mix_sample_kernel(u_ref, x1_ref, cache_ref, lstate_ref, h0_ref,
                       conv_w, conv_b, conv_g, lconv_w, lconv_b, w_gates, ba, bx, lam, w_out, post_g,
                       x2_ref, conv_st_ref, lconv_st_ref, h_st_ref):
    steps = u_ref.shape[0]
    past = CONV_WIDTH - 1
    lpast = LRU_CONV_WIDTH - 1

    gp = [cache_ref[j] for j in range(past)]
    gp += [u_ref[t, :, 0:D_CONV] * jax.nn.sigmoid(u_ref[t, :, D_CONV:2 * D_CONV]) for t in range(steps)]
    for j in range(past):
        conv_st_ref[j] = gp[j + steps]
    c_rows = []
    for t in range(steps):
        acc = jnp.broadcast_to(conv_b[...], gp[0].shape)
        for k in range(CONV_WIDTH):
            acc = acc + conv_w[k:k + 1, :] * gp[t + k]
        cn = _rms(acc, conv_g[...])
        c_rows.append(cn * jax.nn.sigmoid(cn))
    c = jnp.concatenate(c_rows, axis=0).astype(_bf16)

    xp = [lstate_ref[j] for j in range(lpast)]
    xp += [u_ref[t, :, 2 * D_CONV:2 * D_CONV + D_LRU] for t in range(steps)]
    for j in range(lpast):
        lconv_st_ref[j] = xp[j + steps]
    xc_rows = []
    for t in range(steps):
        acc = jnp.broadcast_to(lconv_b[...], xp[0].shape)
        for k in range(LRU_CONV_WIDTH):
            acc = acc + lconv_w[k:k + 1, :] * xp[t + k]
        xc_rows.append(acc)
    xc = jnp.concatenate(xc_rows, axis=0)
    r, i_g = _lru_gates(xc, w_gates[...], ba[...], bx[...])
    a, b = _lru_ab(xc, r, i_g, lam[...])
    bt = h0_ref.shape[0]
    h = h0_ref[...]
    hs = []
    for t in range(steps):
        h = a[t * bt:(t + 1) * bt, :] * h + b[t * bt:(t + 1) * bt, :]
        hs.append(h)
    h_st_ref[...] = h
    u_gelu = jnp.concatenate([u_ref[t, :, 2 * D_CONV + D_LRU:D_IN] for t in range(steps)], axis=0)
    yb = (jnp.concatenate(hs, axis=0) * jax.nn.gelu(u_gelu)).astype(_bf16)

    out = _dot(c, w_out[0:D_CONV, :]) + _dot(yb, w_out[D_CONV:D_CONV + D_LRU, :])
    x2 = _rms(out, post_g[...])
    for t in range(steps):
        x2_ref[t] = x1_ref[t] + x2[t * bt:(t + 1) * bt, :]


def _mix_sample_call(u, x1, cache, lstate, h0, conv_w, conv_b, conv_g, lconv_w, lconv_b, w_gates, ba, bx,
                     lam, w_out, post_g):
    steps, bsz, _ = u.shape
    bt = SAMPLE_BATCH_TILE
    consts = (conv_w, conv_b, conv_g, lconv_w, lconv_b, w_gates, ba, bx, lam, w_out, post_g)
    return pl.pallas_call(
        _mix_sample_kernel,
        grid=(bsz // bt,),
        in_specs=[pl.BlockSpec((steps, bt, D_IN), lambda i: (0, i, 0)),
                  pl.BlockSpec((steps, bt, D_MODEL), lambda i: (0, i, 0)),
                  pl.BlockSpec((CONV_WIDTH - 1, bt, D_CONV), lambda i: (0, i, 0)),
                  pl.BlockSpec((LRU_CONV_WIDTH - 1, bt, D_LRU), lambda i: (0, i, 0)),
                  pl.BlockSpec((bt, D_LRU), lambda i: (i, 0))]
                 + [_const_spec(c.shape) for c in consts],
        out_specs=[pl.BlockSpec((steps, bt, D_MODEL), lambda i: (0, i, 0)),
                   pl.BlockSpec((CONV_WIDTH - 1, bt, D_CONV), lambda i: (0, i, 0)),
                   pl.BlockSpec((LRU_CONV_WIDTH - 1, bt, D_LRU), lambda i: (0, i, 0)),
                   pl.BlockSpec((bt, D_LRU), lambda i: (i, 0))],
        out_shape=[jax.ShapeDtypeStruct((steps, bsz, D_MODEL), _f32),
                   jax.ShapeDtypeStruct((CONV_WIDTH - 1, bsz, D_CONV), _f32),
                   jax.ShapeDtypeStruct((LRU_CONV_WIDTH - 1, bsz, D_LRU), _f32),
                   jax.ShapeDtypeStruct((bsz, D_LRU), _f32)],
        compiler_params=pltpu.CompilerParams(dimension_semantics=("arbitrary",),
                                             vmem_limit_bytes=VMEM_LIMIT_BYTES),
        name="mix_sample",
    )(u, x1, cache, lstate, h0, *consts)


def _block_diag(w):
    heads, d, _ = w.shape
    tiled = jnp.tile(w.reshape(heads * d, d), (1, heads))
    rows = lax.broadcasted_iota(jnp.int32, tiled.shape, 0) // d
    cols = lax.broadcasted_iota(jnp.int32, tiled.shape, 1) // d
    return jnp.where(rows == cols, tiled, 0.0)


def kernel(x_prompt, x_sample, cache_conv, state_lru_conv, state_lru_h, p_prompt, p_sample, ffn1_pre_g, ffn1_post_g, ffn1_w_gu, ffn1_w_down, mix_pre_g, mix_post_g, w_in, conv_w, conv_b, conv_norm_g, lru_conv_w, lru_conv_b, lru_wa, lru_ba, lru_wx, lru_bx, lru_lambda, w_out, ffn2_pre_g, ffn2_post_g, ffn2_w_gu, ffn2_w_down, ple_norm_g, ple_w_gate, ple_w_proj, ple_post_g):
    depth = ffn1_w_gu.shape[0]
    bsz, seq, _ = x_prompt.shape
    dbsz, dseq, _ = x_sample.shape

    xp = x_prompt.reshape(bsz * seq, D_MODEL)
    xs = x_sample.transpose(1, 0, 2).reshape(dseq * dbsz, D_MODEL)
    outs = [[] for _ in range(6)]
    for l in range(depth):
        row = lambda v: v[l:l + 1]
        wgu1, wd1 = ffn1_w_gu[l].astype(_bf16), ffn1_w_down[l].astype(_bf16)
        wgu2, wd2 = ffn2_w_gu[l].astype(_bf16), ffn2_w_down[l].astype(_bf16)
        w_in_l, w_out_l = w_in[l].astype(_bf16), w_out[l].astype(_bf16)
        w_gate_l, w_proj_l = ple_w_gate[l].astype(_bf16), ple_w_proj[l].astype(_bf16)
        w_gates = jnp.concatenate([_block_diag(lru_wa[l]), _block_diag(lru_wx[l])], axis=1).astype(_bf16)
        conv_consts = (conv_w[l], row(conv_b), row(conv_norm_g))
        lru_consts = (lru_conv_w[l], row(lru_conv_b), w_gates, row(lru_ba), row(lru_bx), row(lru_lambda))
        ffn1_args = (row(ffn1_pre_g), row(ffn1_post_g), wgu1, wd1, row(mix_pre_g), w_in_l)
        ffn2_args = (row(ffn2_pre_g), row(ffn2_post_g), wgu2, wd2, row(ple_norm_g), w_gate_l, w_proj_l,
                     row(ple_post_g))

        x2, cst, lst, hst, x1, u = _prompt_a_call(xp, xs, bsz, *ffn1_args, *conv_consts, *lru_consts, w_out_l,
                                                  row(mix_post_g))
        outs[0].append(cst); outs[1].append(lst); outs[2].append(hst.reshape(bsz, D_LRU))

        x2s, cst, lst, hst = _mix_sample_call(u.reshape(dseq, dbsz, D_IN), x1.reshape(dseq, dbsz, D_MODEL),
                                              cache_conv[l].transpose(1, 0, 2),
                                              state_lru_conv[l].transpose(1, 0, 2), state_lru_h[l],
                                              *conv_consts, *lru_consts, w_out_l, row(mix_post_g))
        outs[3].append(cst.transpose(1, 0, 2)); outs[4].append(lst.transpose(1, 0, 2)); outs[5].append(hst)

        ps = p_sample[l].transpose(1, 0, 2).reshape(dseq * dbsz, D_PLE)
        xp = _ffn_ple_call(x2, p_prompt[l].reshape(bsz * seq, D_PLE), *ffn2_args)
        xs = _ffn_ple_call(x2s.reshape(dseq * dbsz, D_MODEL), ps, *ffn2_args)

    y_prompt = xp.reshape(bsz, seq, D_MODEL)
    y_sample = xs.reshape(dseq, dbsz, D_MODEL).transpose(1, 0, 2)
    return (y_prompt, y_sample) + tuple(o[0][None] if depth == 1 else jnp.stack(o) for o in outs)
```

```python
import functools

import jax
import jax.numpy as jnp
from jax import lax
from jax.experimental import pallas as pl
from jax.experimental.pallas import tpu as pltpu

D_MODEL = 1024
D_CONV = 512
D_LRU = 512
D_IN = 2 * D_CONV + 2 * D_LRU
D_FF = 2816
D_PLE = 256
LRU_HEADS = 8
CONV_WIDTH = 31
LRU_CONV_WIDTH = 4
LRU_C = 8.0
EPS = 1e-6
FFN_RES_WEIGHT = 0.5

SUBLANES = 8
LANES = 128
MXU_TILE = 256
VMEM_LIMIT_BYTES = 56 * 1024 * 1024

TOKEN_TILE = 1024
TIME_TILE = 512
CONV_ROWS = 64
CONV_PAD = 32
LCONV_PAD = 8
SAMPLE_BATCH_TILE = 64
ANCHOR_ROWS = 16

FF_CHUNKS = tuple((lo, min(512, D_FF - lo)) for lo in range(0, D_FF, 512))

FF_COLS = 256
DOWN_COLS = 256
IN_COLS = 512
PART_ROWS = 128
PIECES_PER_FF = ("CCCC", "LC", "LC", "LC", "LC", "LC", "LC", "LC", "LC", "LC", "LC")
PIECES_PER_DOWN = ("", "LLCCC", "LLCCC", "LLCC")
PIECES_PER_IN = ("LLLLCCCC", "LCCCCCC", "LLLLCCCC", "F")

_bf16 = jnp.bfloat16
_f32 = jnp.float32


def _rms(x, g):
    ms = jnp.mean(x * x, axis=-1, keepdims=True)
    return x * lax.rsqrt(ms + EPS) * g


def _dot(a, b):
    return jnp.dot(a, b, preferred_element_type=_f32)


def _ffn_ple_stages(x_ref, p_ref, pre_g, post_g, wgu_ref, wd_ref, ple_g, w_gate, w_proj, ple_post_g, y_ref,
                    act_ref):
    x = x_ref[...]
    h = _rms(x, pre_g[...]).astype(_bf16)
    yield
    for lo, n in FF_CHUNKS:
        gate = _dot(h, wgu_ref[:, lo:lo + n])
        up = _dot(h, wgu_ref[:, D_FF + lo:D_FF + lo + n])
        act_ref[:, lo:lo + n] = (gate * jax.nn.sigmoid(gate) * up).astype(_bf16)
    yield
    x3 = x + FFN_RES_WEIGHT * _rms(_dot(act_ref[...], wd_ref[...]), post_g[...])
    yield
    gate = jax.nn.sigmoid(_dot(_rms(x3, ple_g[...]).astype(_bf16), w_gate[...]))
    e = _dot(p_ref[...].astype(_bf16), w_proj[...])
    yield
    y_ref[...] = x3 + _rms(gate * e, ple_post_g[...])
    yield


HALF_STAGE_ORDER = "AABABABABB"


def _ffn_ple_kernel(x_ref, p_ref, pre_g, post_g, wgu, wd, ple_g, w_gate, w_proj, ple_post_g,
                    y_ref, act_ref):
    half = x_ref.shape[0] // 2
    stages = {name: _ffn_ple_stages(x_ref.at[lo:lo + half], p_ref.at[lo:lo + half], pre_g, post_g, wgu, wd, ple_g,
                                    w_gate, w_proj, ple_post_g, y_ref.at[lo:lo + half], act_ref.at[lo:lo + half])
              for name, lo in (("A", 0), ("B", half))}
    for name in HALF_STAGE_ORDER:
        next(stages[name])


def _const_spec(shape):
    zeros = (0,) * len(shape)
    return pl.BlockSpec(shape, lambda *_: zeros, pipeline_mode=pl.Buffered(1))


def _row_spec(tile, width):
    return pl.BlockSpec((tile, width), lambda i: (i, 0))


def _ffn_ple_call(x, p, pre_g, post_g, wgu, wd, ple_g, w_gate, w_proj, ple_post_g):
    n = x.shape[0]
    tile = min(TOKEN_TILE, n)
    return pl.pallas_call(
        _ffn_ple_kernel,
        grid=(n // tile,),
        in_specs=[_row_spec(tile, D_MODEL), _row_spec(tile, D_PLE),
                  _const_spec(pre_g.shape), _const_spec(post_g.shape),
                  _const_spec(wgu.shape), _const_spec(wd.shape), _const_spec(ple_g.shape),
                  _const_spec(w_gate.shape), _const_spec(w_proj.shape), _const_spec(ple_post_g.shape)],
        out_specs=_row_spec(tile, D_MODEL),
        out_shape=jax.ShapeDtypeStruct((n, D_MODEL), _f32),
        scratch_shapes=[pltpu.VMEM((tile, D_FF), _bf16)],
        compiler_params=pltpu.CompilerParams(dimension_semantics=("arbitrary",),
                                             vmem_limit_bytes=VMEM_LIMIT_BYTES),
        name="ffn2_ple",
    )(x, p, pre_g, post_g, wgu, wd, ple_g, w_gate, w_proj, ple_post_g)


def _lru_gate_logits(xc, w_gates_ref):
    assert MXU_TILE % (D_LRU // LRU_HEADS) == 0
    xb = xc.astype(_bf16)
    cols = []
    for gate in range(2):
        for lo in range(0, D_LRU, MXU_TILE):
            cols.append(_dot(xb[:, lo:lo + MXU_TILE],
                             w_gates_ref[lo:lo + MXU_TILE, gate * D_LRU + lo:gate * D_LRU + lo + MXU_TILE]))
    return jnp.concatenate(cols, axis=1)


def _lru_gates(logits, ba, bx):
    return jax.nn.sigmoid(logits[:, :D_LRU] + ba), jax.nn.sigmoid(logits[:, D_LRU:] + bx)


def _lru_ab(xc, r, i_g, lam):
    neg_lam = -lam
    softplus = jnp.maximum(neg_lam, 0.0) + jnp.log1p(jnp.exp(-jnp.abs(neg_lam)))
    log_a = -LRU_C * r * softplus
    a = jnp.exp(log_a)
    b = jnp.sqrt(-jnp.tanh(log_a) * (a * a + 1.0)) * (i_g * xc)
    return a, b


def _causal_taps(buf, r0, c0, rows, pad, width, w_ref, b_ref):
    first = pad - (width - 1)
    x = buf[r0:r0 + rows + pad, c0:c0 + LANES]
    acc = jnp.broadcast_to(b_ref[:, c0:c0 + LANES], (rows, LANES))
    for phase in range(SUBLANES):
        taps = [(q, q * SUBLANES + phase - first) for q in range(pad // SUBLANES + 1)]
        taps = [(q, k) for q, k in taps if 0 <= k < width]
        if not taps:
            continue
        xs = x if phase == 0 else pltpu.roll(x, rows + pad - phase, axis=0)
        for q, k in taps:
            acc = acc + w_ref[k:k + 1, c0:c0 + LANES] * xs[q * SUBLANES:q * SUBLANES + rows, :]
    return acc


def _anchored(value, chains, which):
    pieces = [p for c in which for p in next(chains[c])]
    if callable(value):
        value = value()
    if not pieces:
        return value
    folded = []
    for p in pieces:
        m = jnp.min(p.reshape(-1, SUBLANES, p.shape[-1]), axis=0)
        folded += [m[:, c0:c0 + LANES] for c0 in range(0, m.shape[1], LANES)]
    zero = jnp.minimum(jnp.abs(functools.reduce(jnp.minimum, folded)), 0.0)
    zero = jnp.concatenate([zero] * (ANCHOR_ROWS // SUBLANES), axis=0).astype(value.dtype)
    zero = jnp.concatenate([zero] * (value.shape[1] // LANES), axis=1)
    split = value.shape[0] - ANCHOR_ROWS
    return jnp.concatenate([value[0:split], value[split:] + zero], axis=0)


def _ffn_anchored(x, pre_g, post_g, wgu_ref, wd_ref, act_ref, chains, per_ff, per_down):
    h = _rms(x, pre_g).astype(_bf16)
    for lo, which in zip(range(0, D_FF, FF_COLS), per_ff, strict=True):
        gate = _dot(h, wgu_ref[:, lo:lo + FF_COLS])
        up = _dot(h, wgu_ref[:, D_FF + lo:D_FF + lo + FF_COLS])
        act = _anchored(lambda: gate * jax.nn.sigmoid(gate) * up, chains, which)
        act_ref[:, lo:lo + FF_COLS] = act.astype(_bf16)
    ys = [_dot(_anchored(act_ref[...], chains, which), wd_ref[:, lo:lo + DOWN_COLS])
          for lo, which in zip(range(0, D_MODEL, DOWN_COLS), per_down, strict=True)]
    return x + FFN_RES_WEIGHT * _rms(jnp.concatenate(ys, axis=1), post_g)


def _lru_steps(fresh, lconv_w, lconv_b, w_gates, ba, bx, lam, ubuf, lconv_st_ref, h_st_ref,
               xbuf, xcbuf, hbuf, ybuf, hcar):
    tt = ubuf.shape[0]
    parts = [(lo, lo + PART_ROWS) for lo in range(0, tt, PART_ROWS)]

    for lo, hi in parts:
        blocks = []
        for r0 in range(lo, hi, CONV_ROWS):
            for c0 in range(0, D_LRU, LANES):
                xc = _causal_taps(xbuf, r0, c0, CONV_ROWS, LCONV_PAD, LRU_CONV_WIDTH, lconv_w, lconv_b)
                xcbuf[r0:r0 + CONV_ROWS, c0:c0 + LANES] = xc
                blocks.append(xc)
        yield blocks
    lconv_st_ref[0] = xbuf[pl.ds(LCONV_PAD + tt - (LRU_CONV_WIDTH - 1), LRU_CONV_WIDTH - 1), :]

    gates = []
    for lo, hi in parts:
        xc = xcbuf[lo:hi, :]
        r, i_g = _lru_gates(_lru_gate_logits(xc, w_gates), ba[...], bx[...])
        gates.append((xc, r, i_g))
        yield [r, i_g]
    coeffs = []
    for xc, r, i_g in gates:
        a, b = _lru_ab(xc, r, i_g, lam[...])
        coeffs.append((a, b))
        yield [b]

    scanned = []
    for a, b in coeffs:
        for lo in range(0, PART_ROWS, CONV_ROWS):
            aq = a[lo:lo + CONV_ROWS].reshape(CONV_ROWS // SUBLANES, SUBLANES, D_LRU)
            bq = b[lo:lo + CONV_ROWS].reshape(aq.shape)
            row = lax.broadcasted_iota(jnp.int32, aq.shape, 1)
            shift = 1
            while shift < SUBLANES:
                keep = row >= shift
                bq = jnp.where(keep, aq * pltpu.roll(bq, shift, axis=1) + bq, bq)
                aq = jnp.where(keep, aq * pltpu.roll(aq, shift, axis=1), aq)
                shift *= 2
            scanned.append((aq, bq))
            yield [aq, bq]

    carry = jnp.where(fresh, 0.0, hcar[...])
    r0 = 0
    for aq, bq in scanned:
        for i in range(aq.shape[0]):
            rows = aq[i] * carry + bq[i]
            hbuf[r0:r0 + SUBLANES, :] = rows
            carry = rows[SUBLANES - 1:SUBLANES, :]
            r0 += SUBLANES
    hcar[...] = carry
    h_st_ref[0] = carry
    yield [rows]

    for lo, hi in parts:
        yb = hbuf[lo:hi, :] * jax.nn.gelu(ubuf[lo:hi, 2 * D_CONV + D_LRU:D_IN])
        ybuf[lo:hi, :] = yb.astype(_bf16)
        yield [yb]


def _conv_steps(conv_w, conv_b, conv_g, ubuf, conv_st_ref, gbuf, cbuf):
    tt = cbuf.shape[0]
    for lo in range(0, tt, PART_ROWS):
        g = ubuf[lo:lo + PART_ROWS, 0:D_CONV] * jax.nn.sigmoid(ubuf[lo:lo + PART_ROWS, D_CONV:2 * D_CONV])
        gbuf[CONV_PAD + lo:CONV_PAD + lo + PART_ROWS, :] = g
        yield [g]
    conv_st_ref[0] = gbuf[pl.ds(CONV_PAD + tt - (CONV_WIDTH - 1), CONV_WIDTH - 1), :]
    for r0 in range(0, tt, CONV_ROWS):
        cols = []
        for c0 in range(0, D_CONV, LANES):
            cols.append(_causal_taps(gbuf, r0, c0, CONV_ROWS, CONV_PAD, CONV_WIDTH, conv_w, conv_b))
            if len(cols) * LANES < D_CONV:
                yield [cols[-1]]
        cn = _rms(jnp.concatenate(cols, axis=1), conv_g[...])
        c = cn * jax.nn.sigmoid(cn)
        cbuf[r0:r0 + CONV_ROWS, :] = c.astype(_bf16)
        yield [c]


def _out_steps(w_out, post_g, x2_ref, cbuf, ybuf):
    out = _dot(cbuf[...], w_out[0:D_CONV, :]) + _dot(ybuf[...], w_out[D_CONV:D_CONV + D_LRU, :])
    delta = _rms(out, post_g[...])
    x2_ref[...] += delta
    yield [delta]


def _prompt_a_kernel(x_ref, xs_ref, pre_g, post_g, wgu, wd, mix_g, w_in,
                     conv_w, conv_b, conv_g, lconv_w, lconv_b, w_gates, ba, bx, lam, w_out, mix_post_g,
                     x2_ref, conv_st_ref, lconv_st_ref, h_st_ref, x1s_hbm, us_hbm,
                     act_ref, ubuf, x1buf, gbuf, xbuf, cbuf, xcbuf, hbuf, ybuf, hcar, sems, *, tiles_per_seq):
    s = pl.program_id(0)
    last = pl.num_programs(0) - 1
    tt = x_ref.shape[0]

    @pl.when(s == 0)
    def _():
        ubuf[...] = jnp.zeros(ubuf.shape, _f32)
        x1buf[...] = jnp.zeros(x1buf.shape, _f32)
        gbuf[...] = jnp.zeros(gbuf.shape, _f32)
        xbuf[...] = jnp.zeros(xbuf.shape, _f32)
        hcar[...] = jnp.zeros(hcar.shape, _f32)

    fresh = lax.rem(s + tiles_per_seq - 1, tiles_per_seq) == 0
    x2_ref[...] = x1buf[...]
    gbuf[0:CONV_PAD, :] = jnp.where(fresh, 0.0, gbuf[tt:tt + CONV_PAD, :])
    xbuf[0:LCONV_PAD, :] = jnp.where(fresh, 0.0, xbuf[tt:tt + LCONV_PAD, :])
    xbuf[LCONV_PAD:LCONV_PAD + tt, :] = ubuf[:, 2 * D_CONV:2 * D_CONV + D_LRU]
    chains = {
        "L": _lru_steps(fresh, lconv_w, lconv_b, w_gates, ba, bx, lam, ubuf, lconv_st_ref, h_st_ref,
                        xbuf, xcbuf, hbuf, ybuf, hcar),
        "C": _conv_steps(conv_w, conv_b, conv_g, ubuf, conv_st_ref, gbuf, cbuf),
        "F": _out_steps(w_out, mix_post_g, x2_ref, cbuf, ybuf),
    }

    x = jnp.where(s == last, xs_ref[...], x_ref[...])
    x1 = _ffn_anchored(x, pre_g[...], post_g[...], wgu, wd, act_ref, chains, PIECES_PER_FF, PIECES_PER_DOWN)
    x1buf[...] = x1
    hm = _rms(x1, mix_g[...]).astype(_bf16)
    for lo, which in zip(range(0, D_IN, IN_COLS), PIECES_PER_IN, strict=True):
        ubuf[:, lo:lo + IN_COLS] = _anchored(_dot(hm, w_in[:, lo:lo + IN_COLS]), chains, which)
    assert all(next(chain, None) is None for chain in chains.values())

    @pl.when(s == last)
    def _():
        copies = [pltpu.make_async_copy(x1buf, x1s_hbm, sems.at[0]),
                  pltpu.make_async_copy(ubuf, us_hbm, sems.at[1])]
        for copy in copies:
            copy.start()
        for copy in copies:
            copy.wait()


def _prompt_a_call(x, xs, bsz, pre_g, post_g, wgu, wd, mix_g, w_in, conv_w, conv_b, conv_g, lconv_w, lconv_b,
                   w_gates, ba, bx, lam, w_out, mix_post_g):
    n = x.shape[0]
    tt = TIME_TILE
    tiles = n // tt
    tiles_per_seq = tiles // bsz
    assert xs.shape == (tt, D_MODEL), "the sample group must fill exactly one tile of the first chain"
    consts = (pre_g, post_g, wgu, wd, mix_g, w_in, conv_w, conv_b, conv_g, lconv_w, lconv_b, w_gates, ba, bx,
              lam, w_out, mix_post_g)
    prev_seq = lambda s: (jnp.maximum(s - 1, 0) // tiles_per_seq, 0, 0)
    return pl.pallas_call(
        functools.partial(_prompt_a_kernel, tiles_per_seq=tiles_per_seq),
        grid=(tiles + 1,),
        in_specs=[pl.BlockSpec((tt, D_MODEL), lambda s: (jnp.minimum(s, tiles - 1), 0)), _const_spec(xs.shape)]
                 + [_const_spec(c.shape) for c in consts],
        out_specs=[pl.BlockSpec((tt, D_MODEL), lambda s: (jnp.maximum(s - 1, 0), 0)),
                   pl.BlockSpec((1, CONV_WIDTH - 1, D_CONV), prev_seq),
                   pl.BlockSpec((1, LRU_CONV_WIDTH - 1, D_LRU), prev_seq),
                   pl.BlockSpec((1, 1, D_LRU), prev_seq),
                   pl.BlockSpec(memory_space=pl.ANY),
                   pl.BlockSpec(memory_space=pl.ANY)],
        out_shape=[jax.ShapeDtypeStruct((n, D_MODEL), _f32),
                   jax.ShapeDtypeStruct((bsz, CONV_WIDTH - 1, D_CONV), _f32),
                   jax.ShapeDtypeStruct((bsz, LRU_CONV_WIDTH - 1, D_LRU), _f32),
                   jax.ShapeDtypeStruct((bsz, 1, D_LRU), _f32),
                   jax.ShapeDtypeStruct((tt, D_MODEL), _f32),
                   jax.ShapeDtypeStruct((tt, D_IN), _f32)],
        scratch_shapes=[pltpu.VMEM((tt, D_FF), _bf16),
                        pltpu.VMEM((tt, D_IN), _f32),
                        pltpu.VMEM((tt, D_MODEL), _f32),
                        pltpu.VMEM((CONV_PAD + tt, D_CONV), _f32),
                        pltpu.VMEM((LCONV_PAD + tt, D_LRU), _f32),
                        pltpu.VMEM((tt, D_CONV), _bf16),
                        pltpu.VMEM((tt, D_LRU), _f32),
                        pltpu.VMEM((tt, D_LRU), _f32),
                        pltpu.VMEM((tt, D_LRU), _bf16),
                        pltpu.VMEM((1, D_LRU), _f32),
                        pltpu.SemaphoreType.DMA((2,))],
        compiler_params=pltpu.CompilerParams(dimension_semantics=("arbitrary",),
                                             vmem_limit_bytes=VMEM_LIMIT_BYTES),
        name="prompt_ffn1_mix",
    )(x, xs, *consts)


def _mix_sample_kernel(u_ref, x1_ref, cache_ref, lstate_ref, h0_ref,
                       conv_w, conv_b, conv_g, lconv_w, lconv_b, w_gates, ba, bx, lam, w_out, post_g,
                       x2_ref, conv_st_ref, lconv_st_ref, h_st_ref):
    steps = u_ref.shape[0]
    past = CONV_WIDTH - 1
    lpast = LRU_CONV_WIDTH - 1

    gp = [cache_ref[j] for j in range(past)]
    gp += [u_ref[t, :, 0:D_CONV] * jax.nn.sigmoid(u_ref[t, :, D_CONV:2 * D_CONV]) for t in range(steps)]
    for j in range(past):
        conv_st_ref[j] = gp[j + steps]
    c_rows = []
    for t in range(steps):
        acc = jnp.broadcast_to(conv_b[...], gp[0].shape)
        for k in range(CONV_WIDTH):
            acc = acc + conv_w[k:k + 1, :] * gp[t + k]
        cn = _rms(acc, conv_g[...])
        c_rows.append(cn * jax.nn.sigmoid(cn))
    c = jnp.concatenate(c_rows, axis=0).astype(_bf16)

    xp = [lstate_ref[j] for j in range(lpast)]
    xp += [u_ref[t, :, 2 * D_CONV:2 * D_CONV + D_LRU] for t in range(steps)]
    for j in range(lpast):
        lconv_st_ref[j] = xp[j + steps]
    xc_rows = []
    for t in range(steps):
        acc = jnp.broadcast_to(lconv_b[...], xp[0].shape)
        for k in range(LRU_CONV_WIDTH):
            acc = acc + lconv_w[k:k + 1, :] * xp[t + k]
        xc_rows.append(acc)
    xc = jnp.concatenate(xc_rows, axis=0)
    r, i_g = _lru_gates(_lru_gate_logits(xc, w_gates), ba[...], bx[...])
    a, b = _lru_ab(xc, r, i_g, lam[...])
    bt = h0_ref.shape[0]
    h = h0_ref[...]
    hs = []
    for t in range(steps):
        h = a[t * bt:(t + 1) * bt, :] * h + b[t * bt:(t + 1) * bt, :]
        hs.append(h)
    h_st_ref[...] = h
    u_gelu = jnp.concatenate([u_ref[t, :, 2 * D_CONV + D_LRU:D_IN] for t in range(steps)], axis=0)
    yb = (jnp.concatenate(hs, axis=0) * jax.nn.gelu(u_gelu)).astype(_bf16)

    out = _dot(c, w_out[0:D_CONV, :]) + _dot(yb, w_out[D_CONV:D_CONV + D_LRU, :])
    x2 = _rms(out, post_g[...])
    for t in range(steps):
        x2_ref[t] = x1_ref[t] + x2[t * bt:(t + 1) * bt, :]


def _mix_sample_call(u, x1, cache, lstate, h0, conv_w, conv_b, conv_g, lconv_w, lconv_b, w_gates, ba, bx,
                     lam, w_out, post_g):
    steps, bsz, _ = u.shape
    bt = SAMPLE_BATCH_TILE
    consts = (conv_w, conv_b, conv_g, lconv_w, lconv_b, w_gates, ba, bx, lam, w_out, post_g)
    return pl.pallas_call(
        _mix_sample_kernel,
        grid=(bsz // bt,),
        in_specs=[pl.BlockSpec((steps, bt, D_IN), lambda i: (0, i, 0)),
                  pl.BlockSpec((steps, bt, D_MODEL), lambda i: (0, i, 0)),
                  pl.BlockSpec((CONV_WIDTH - 1, bt, D_CONV), lambda i: (0, i, 0)),
                  pl.BlockSpec((LRU_CONV_WIDTH - 1, bt, D_LRU), lambda i: (0, i, 0)),
                  pl.BlockSpec((bt, D_LRU), lambda i: (i, 0))]
                 + [_const_spec(c.shape) for c in consts],
        out_specs=[pl.BlockSpec((steps, bt, D_MODEL), lambda i: (0, i, 0)),
                   pl.BlockSpec((CONV_WIDTH - 1, bt, D_CONV), lambda i: (0, i, 0)),
                   pl.BlockSpec((LRU_CONV_WIDTH - 1, bt, D_LRU), lambda i: (0, i, 0)),
                   pl.BlockSpec((bt, D_LRU), lambda i: (i, 0))],
        out_shape=[jax.ShapeDtypeStruct((steps, bsz, D_MODEL), _f32),
                   jax.ShapeDtypeStruct((CONV_WIDTH - 1, bsz, D_CONV), _f32),
                   jax.ShapeDtypeStruct((LRU_CONV_WIDTH - 1, bsz, D_LRU), _f32),
                   jax.ShapeDtypeStruct((bsz, D_LRU), _f32)],
        compiler_params=pltpu.CompilerParams(dimension_semantics=("arbitrary",),
                                             vmem_limit_bytes=VMEM_LIMIT_BYTES),
        name="mix_sample",
    )(u, x1, cache, lstate, h0, *consts)


def _block_diag(w):
    heads, d, _ = w.shape
    tiled = jnp.tile(w.reshape(heads * d, d), (1, heads))
    rows = lax.broadcasted_iota(jnp.int32, tiled.shape, 0) // d
    cols = lax.broadcasted_iota(jnp.int32, tiled.shape, 1) // d
    return jnp.where(rows == cols, tiled, 0.0)


def kernel(x_prompt, x_sample, cache_conv, state_lru_conv, state_lru_h, p_prompt, p_sample, ffn1_pre_g, ffn1_post_g, ffn1_w_gu, ffn1_w_down, mix_pre_g, mix_post_g, w_in, conv_w, conv_b, conv_norm_g, lru_conv_w, lru_conv_b, lru_wa, lru_ba, lru_wx, lru_bx, lru_lambda, w_out, ffn2_pre_g, ffn2_post_g, ffn2_w_gu, ffn2_w_down, ple_norm_g, ple_w_gate, ple_w_proj, ple_post_g):
    depth = ffn1_w_gu.shape[0]
    bsz, seq, _ = x_prompt.shape
    dbsz, dseq, _ = x_sample.shape

    xp = x_prompt.reshape(bsz * seq, D_MODEL)
    xs = x_sample.transpose(1, 0, 2).reshape(dseq * dbsz, D_MODEL)
    outs = [[] for _ in range(6)]
    for l in range(depth):
        row = lambda v: v[l:l + 1]
        wgu1, wd1 = ffn1_w_gu[l].astype(_bf16), ffn1_w_down[l].astype(_bf16)
        wgu2, wd2 = ffn2_w_gu[l].astype(_bf16), ffn2_w_down[l].astype(_bf16)
        w_in_l, w_out_l = w_in[l].astype(_bf16), w_out[l].astype(_bf16)
        w_gate_l, w_proj_l = ple_w_gate[l].astype(_bf16), ple_w_proj[l].astype(_bf16)
        w_gates = jnp.concatenate([_block_diag(lru_wa[l]), _block_diag(lru_wx[l])], axis=1).astype(_bf16)
        conv_consts = (conv_w[l], row(conv_b), row(conv_norm_g))
        lru_consts = (lru_conv_w[l], row(lru_conv_b), w_gates, row(lru_ba), row(lru_bx), row(lru_lambda))
        ffn1_args = (row(ffn1_pre_g), row(ffn1_post_g), wgu1, wd1, row(mix_pre_g), w_in_l)
        ffn2_args = (row(ffn2_pre_g), row(ffn2_post_g), wgu2, wd2, row(ple_norm_g), w_gate_l, w_proj_l,
                     row(ple_post_g))

        x2, cst, lst, hst, x1, u = _prompt_a_call(xp, xs, bsz, *ffn1_args, *conv_consts, *lru_consts, w_out_l,
                                                  row(mix_post_g))
        outs[0].append(cst); outs[1].append(lst); outs[2].append(hst.reshape(bsz, D_LRU))

        x2s, cst, lst, hst = _mix_sample_call(u.reshape(dseq, dbsz, D_IN), x1.reshape(dseq, dbsz, D_MODEL),
                                              cache_conv[l].transpose(1, 0, 2),
                                              state_lru_conv[l].transpose(1, 0, 2), state_lru_h[l],
                                              *conv_consts, *lru_consts, w_out_l, row(mix_post_g))
        outs[3].append(cst.transpose(1, 0, 2)); outs[4].append(lst.transpose(1, 0, 2)); outs[5].append(hst)

        ps = p_sample[l].transpose(1, 0, 2).reshape(dseq * dbsz, D_PLE)
        xp = _ffn_ple_call(x2, p_prompt[l].reshape(bsz * seq, D_PLE), *ffn2_args)
        xs = _ffn_ple_call(x2s.reshape(dseq * dbsz, D_MODEL), ps, *ffn2_args)

    y_prompt = xp.reshape(bsz, seq, D_MODEL)
    y_sample = xs.reshape(dseq, dbsz, D_MODEL).transpose(1, 0, 2)
    return (y_prompt, y_sample) + tuple(o[0][None] if depth == 1 else jnp.stack(o) for o in outs)
```

```python
import functools

import jax
import jax.numpy as jnp
from jax import lax
from jax.experimental import pallas as pl
from jax.experimental.pallas import tpu as pltpu

D_MODEL = 1024
D_CONV = 512
D_LRU = 512
D_IN = 2 * D_CONV + 2 * D_LRU
D_FF = 2816
D_PLE = 256
LRU_HEADS = 8
CONV_WIDTH = 31
LRU_CONV_WIDTH = 4
LRU_C = 8.0
EPS = 1e-6
FFN_RES_WEIGHT = 0.5

SUBLANES = 8
LANES = 128
MXU_TILE = 256
VMEM_LIMIT_BYTES = 56 * 1024 * 1024
FFN_PLE_VMEM_LIMIT_BYTES = 60 * 1024 * 1024

TOKEN_TILE = 1024
TIME_TILE = 512
CONV_ROWS = 64
CONV_PAD = 32
LCONV_PAD = 8
SAMPLE_BATCH_TILE = 64
ANCHOR_ROWS = 16

FF_CHUNKS = tuple((lo, min(512, D_FF - lo)) for lo in range(0, D_FF, 512))

FF_COLS = 256
DOWN_COLS = 256
IN_COLS = 512
PART_ROWS = 128
PIECES_PER_FF = ("CCCC", "LC", "LC", "LC", "LC", "LC", "LC", "LC", "LC", "LC", "LC")
PIECES_PER_DOWN = ("", "LLCCC", "LLCCC", "LLCC")
PIECES_PER_IN = ("LLLLCCCC", "LCCCCCC", "LLLLCCCC", "F")

_bf16 = jnp.bfloat16
_f32 = jnp.float32


def _rms(x, g):
    ms = jnp.mean(x * x, axis=-1, keepdims=True)
    return x * lax.rsqrt(ms + EPS) * g


def _dot(a, b):
    return jnp.dot(a, b, preferred_element_type=_f32)


def _ffn_ple_stages(x_ref, p_ref, pre_g, post_g, wgu_ref, wd_ref, ple_g, w_gate, w_proj, ple_post_g, y_ref,
                    act_ref):
    x = x_ref[...]
    h = _rms(x, pre_g[...]).astype(_bf16)
    yield
    for lo, n in FF_CHUNKS:
        gate = _dot(h, wgu_ref[:, lo:lo + n])
        up = _dot(h, wgu_ref[:, D_FF + lo:D_FF + lo + n])
        act_ref[:, lo:lo + n] = (gate * jax.nn.sigmoid(gate) * up).astype(_bf16)
    yield
    x3 = x + FFN_RES_WEIGHT * _rms(_dot(act_ref[...], wd_ref[...]), post_g[...])
    yield
    gate = jax.nn.sigmoid(_dot(_rms(x3, ple_g[...]).astype(_bf16), w_gate[...]))
    e = _dot(p_ref[...].astype(_bf16), w_proj[...])
    yield
    y_ref[...] = x3 + _rms(gate * e, ple_post_g[...])
    yield


HALF_STAGE_ORDER = "AABABABABB"


def _ffn_ple_block(x_ref, p_ref, y_ref, act_ref, weights):
    half = x_ref.shape[0] // 2
    stages = {name: _ffn_ple_stages(x_ref.at[lo:lo + half], p_ref.at[lo:lo + half], *weights,
                                    y_ref.at[lo:lo + half], act_ref.at[lo:lo + half])
              for name, lo in (("A", 0), ("B", half))}
    for name in HALF_STAGE_ORDER:
        next(stages[name])


def _ffn_ple_kernel(x_ref, p_ref, xs_ref, ps_ref, pre_g, post_g, wgu, wd, ple_g, w_gate, w_proj, ple_post_g,
                    y_ref, ys_ref, act_ref):
    weights = (pre_g, post_g, wgu, wd, ple_g, w_gate, w_proj, ple_post_g)
    last = pl.num_programs(0) - 1

    @pl.when(pl.program_id(0) < last)
    def _():
        _ffn_ple_block(x_ref, p_ref, y_ref, act_ref, weights)

    @pl.when(pl.program_id(0) == last)
    def _():
        _ffn_ple_block(xs_ref, ps_ref, ys_ref, act_ref.at[0:xs_ref.shape[0]], weights)


def _const_spec(shape):
    zeros = (0,) * len(shape)
    return pl.BlockSpec(shape, lambda *_: zeros, pipeline_mode=pl.Buffered(1))


def _ffn_ple_call(x, p, xs, ps, pre_g, post_g, wgu, wd, ple_g, w_gate, w_proj, ple_post_g):
    n, ns = x.shape[0], xs.shape[0]
    tile = TOKEN_TILE
    tiles = n // tile
    assert ns <= tile
    consts = (pre_g, post_g, wgu, wd, ple_g, w_gate, w_proj, ple_post_g)
    rows = lambda width: pl.BlockSpec((tile, width), lambda i: (jnp.minimum(i, tiles - 1), 0))
    return pl.pallas_call(
        _ffn_ple_kernel,
        grid=(tiles + 1,),
        in_specs=[rows(D_MODEL), rows(D_PLE), _const_spec(xs.shape), _const_spec(ps.shape)]
                 + [_const_spec(c.shape) for c in consts],
        out_specs=[rows(D_MODEL), pl.BlockSpec((ns, D_MODEL), lambda i: (0, 0))],
        out_shape=[jax.ShapeDtypeStruct((n, D_MODEL), _f32), jax.ShapeDtypeStruct((ns, D_MODEL), _f32)],
        scratch_shapes=[pltpu.VMEM((tile, D_FF), _bf16)],
        compiler_params=pltpu.CompilerParams(dimension_semantics=("arbitrary",),
                                             vmem_limit_bytes=FFN_PLE_VMEM_LIMIT_BYTES),
        name="ffn2_ple",
    )(x, p, xs, ps, *consts)


def _lru_gate_logits(xc, w_gates_ref):
    assert MXU_TILE % (D_LRU // LRU_HEADS) == 0
    xb = xc.astype(_bf16)
    cols = []
    for gate in range(2):
        for lo in range(0, D_LRU, MXU_TILE):
            cols.append(_dot(xb[:, lo:lo + MXU_TILE],
                             w_gates_ref[lo:lo + MXU_TILE, gate * D_LRU + lo:gate * D_LRU + lo + MXU_TILE]))
    return jnp.concatenate(cols, axis=1)


def _lru_gates(logits, ba, bx):
    return jax.nn.sigmoid(logits[:, :D_LRU] + ba), jax.nn.sigmoid(logits[:, D_LRU:] + bx)


def _lru_ab(xc, r, i_g, lam):
    neg_lam = -lam
    softplus = jnp.maximum(neg_lam, 0.0) + jnp.log1p(jnp.exp(-jnp.abs(neg_lam)))
    log_a = -LRU_C * r * softplus
    a = jnp.exp(log_a)
    b = jnp.sqrt(-jnp.tanh(log_a) * (a * a + 1.0)) * (i_g * xc)
    return a, b


def _causal_taps(buf, r0, c0, rows, pad, width, w_ref, b_ref):
    first = pad - (width - 1)
    x = buf[r0:r0 + rows + pad, c0:c0 + LANES]
    acc = jnp.broadcast_to(b_ref[:, c0:c0 + LANES], (rows, LANES))
    for phase in range(SUBLANES):
        taps = [(q, q * SUBLANES + phase - first) for q in range(pad // SUBLANES + 1)]
        taps = [(q, k) for q, k in taps if 0 <= k < width]
        if not taps:
            continue
        xs = x if phase == 0 else pltpu.roll(x, rows + pad - phase, axis=0)
        for q, k in taps:
            acc = acc + w_ref[k:k + 1, c0:c0 + LANES] * xs[q * SUBLANES:q * SUBLANES + rows, :]
    return acc


def _anchored(value, chains, which):
    pieces = [p for c in which for p in next(chains[c])]
    if callable(value):
        value = value()
    if not pieces:
        return value
    folded = []
    for p in pieces:
        m = jnp.min(p.reshape(-1, SUBLANES, p.shape[-1]), axis=0)
        folded += [m[:, c0:c0 + LANES] for c0 in range(0, m.shape[1], LANES)]
    zero = jnp.minimum(jnp.abs(functools.reduce(jnp.minimum, folded)), 0.0)
    zero = jnp.concatenate([zero] * (ANCHOR_ROWS // SUBLANES), axis=0).astype(value.dtype)
    zero = jnp.concatenate([zero] * (value.shape[1] // LANES), axis=1)
    split = value.shape[0] - ANCHOR_ROWS
    return jnp.concatenate([value[0:split], value[split:] + zero], axis=0)


def _ffn_anchored(x, pre_g, post_g, wgu_ref, wd_ref, act_ref, chains, per_ff, per_down):
    h = _rms(x, pre_g).astype(_bf16)
    for lo, which in zip(range(0, D_FF, FF_COLS), per_ff, strict=True):
        gate = _dot(h, wgu_ref[:, lo:lo + FF_COLS])
        up = _dot(h, wgu_ref[:, D_FF + lo:D_FF + lo + FF_COLS])
        act = _anchored(lambda: gate * jax.nn.sigmoid(gate) * up, chains, which)
        act_ref[:, lo:lo + FF_COLS] = act.astype(_bf16)
    ys = [_dot(_anchored(act_ref[...], chains, which), wd_ref[:, lo:lo + DOWN_COLS])
          for lo, which in zip(range(0, D_MODEL, DOWN_COLS), per_down, strict=True)]
    return x + FFN_RES_WEIGHT * _rms(jnp.concatenate(ys, axis=1), post_g)


def _lru_steps(fresh, lconv_w, lconv_b, w_gates, ba, bx, lam, ubuf, lconv_st_ref, h_st_ref,
               xbuf, xcbuf, hbuf, ybuf, hcar):
    tt = ubuf.shape[0]
    parts = [(lo, lo + PART_ROWS) for lo in range(0, tt, PART_ROWS)]

    for lo, hi in parts:
        blocks = []
        for r0 in range(lo, hi, CONV_ROWS):
            for c0 in range(0, D_LRU, LANES):
                xc = _causal_taps(xbuf, r0, c0, CONV_ROWS, LCONV_PAD, LRU_CONV_WIDTH, lconv_w, lconv_b)
                xcbuf[r0:r0 + CONV_ROWS, c0:c0 + LANES] = xc
                blocks.append(xc)
        yield blocks
    lconv_st_ref[0] = xbuf[pl.ds(LCONV_PAD + tt - (LRU_CONV_WIDTH - 1), LRU_CONV_WIDTH - 1), :]

    gates = []
    for lo, hi in parts:
        xc = xcbuf[lo:hi, :]
        r, i_g = _lru_gates(_lru_gate_logits(xc, w_gates), ba[...], bx[...])
        gates.append((xc, r, i_g))
        yield [r, i_g]
    coeffs = []
    for xc, r, i_g in gates:
        a, b = _lru_ab(xc, r, i_g, lam[...])
        coeffs.append((a, b))
        yield [b]

    scanned = []
    for a, b in coeffs:
        for lo in range(0, PART_ROWS, CONV_ROWS):
            aq = a[lo:lo + CONV_ROWS].reshape(CONV_ROWS // SUBLANES, SUBLANES, D_LRU)
            bq = b[lo:lo + CONV_ROWS].reshape(aq.shape)
            row = lax.broadcasted_iota(jnp.int32, aq.shape, 1)
            shift = 1
            while shift < SUBLANES:
                keep = row >= shift
                bq = jnp.where(keep, aq * pltpu.roll(bq, shift, axis=1) + bq, bq)
                aq = jnp.where(keep, aq * pltpu.roll(aq, shift, axis=1), aq)
                shift *= 2
            scanned.append((aq, bq))
            yield [aq, bq]

    carry = jnp.where(fresh, 0.0, hcar[...])
    r0 = 0
    for aq, bq in scanned:
        for i in range(aq.shape[0]):
            rows = aq[i] * carry + bq[i]
            hbuf[r0:r0 + SUBLANES, :] = rows
            carry = rows[SUBLANES - 1:SUBLANES, :]
            r0 += SUBLANES
    hcar[...] = carry
    h_st_ref[0] = carry
    yield [rows]

    for lo, hi in parts:
        yb = hbuf[lo:hi, :] * jax.nn.gelu(ubuf[lo:hi, 2 * D_CONV + D_LRU:D_IN])
        ybuf[lo:hi, :] = yb.astype(_bf16)
        yield [yb]


def _conv_steps(conv_w, conv_b, conv_g, ubuf, conv_st_ref, gbuf, cbuf):
    tt = cbuf.shape[0]
    for lo in range(0, tt, PART_ROWS):
        g = ubuf[lo:lo + PART_ROWS, 0:D_CONV] * jax.nn.sigmoid(ubuf[lo:lo + PART_ROWS, D_CONV:2 * D_CONV])
        gbuf[CONV_PAD + lo:CONV_PAD + lo + PART_ROWS, :] = g
        yield [g]
    conv_st_ref[0] = gbuf[pl.ds(CONV_PAD + tt - (CONV_WIDTH - 1), CONV_WIDTH - 1), :]
    for r0 in range(0, tt, CONV_ROWS):
        cols = []
        for c0 in range(0, D_CONV, LANES):
            cols.append(_causal_taps(gbuf, r0, c0, CONV_ROWS, CONV_PAD, CONV_WIDTH, conv_w, conv_b))
            if len(cols) * LANES < D_CONV:
                yield [cols[-1]]
        cn = _rms(jnp.concatenate(cols, axis=1), conv_g[...])
        c = cn * jax.nn.sigmoid(cn)
        cbuf[r0:r0 + CONV_ROWS, :] = c.astype(_bf16)
        yield [c]


def _out_steps(w_out, post_g, x2_ref, cbuf, ybuf):
    out = _dot(cbuf[...], w_out[0:D_CONV, :]) + _dot(ybuf[...], w_out[D_CONV:D_CONV + D_LRU, :])
    delta = _rms(out, post_g[...])
    x2_ref[...] += delta
    yield [delta]


def _prompt_a_kernel(x_ref, xs_ref, pre_g, post_g, wgu, wd, mix_g, w_in,
                     conv_w, conv_b, conv_g, lconv_w, lconv_b, w_gates, ba, bx, lam, w_out, mix_post_g,
                     x2_ref, conv_st_ref, lconv_st_ref, h_st_ref, x1s_hbm, us_hbm,
                     act_ref, ubuf, x1buf, gbuf, xbuf, cbuf, xcbuf, hbuf, ybuf, hcar, sems, *, tiles_per_seq):
    s = pl.program_id(0)
    last = pl.num_programs(0) - 1
    tt = x_ref.shape[0]

    @pl.when(s == 0)
    def _():
        ubuf[...] = jnp.zeros(ubuf.shape, _f32)
        x1buf[...] = jnp.zeros(x1buf.shape, _f32)
        gbuf[...] = jnp.zeros(gbuf.shape, _f32)
        xbuf[...] = jnp.zeros(xbuf.shape, _f32)
        hcar[...] = jnp.zeros(hcar.shape, _f32)

    fresh = lax.rem(s + tiles_per_seq - 1, tiles_per_seq) == 0
    x2_ref[...] = x1buf[...]
    gbuf[0:CONV_PAD, :] = jnp.where(fresh, 0.0, gbuf[tt:tt + CONV_PAD, :])
    xbuf[0:LCONV_PAD, :] = jnp.where(fresh, 0.0, xbuf[tt:tt + LCONV_PAD, :])
    xbuf[LCONV_PAD:LCONV_PAD + tt, :] = ubuf[:, 2 * D_CONV:2 * D_CONV + D_LRU]
    chains = {
        "L": _lru_steps(fresh, lconv_w, lconv_b, w_gates, ba, bx, lam, ubuf, lconv_st_ref, h_st_ref,
                        xbuf, xcbuf, hbuf, ybuf, hcar),
        "C": _conv_steps(conv_w, conv_b, conv_g, ubuf, conv_st_ref, gbuf, cbuf),
        "F": _out_steps(w_out, mix_post_g, x2_ref, cbuf, ybuf),
    }

    x = jnp.where(s == last, xs_ref[...], x_ref[...])
    x1 = _ffn_anchored(x, pre_g[...], post_g[...], wgu, wd, act_ref, chains, PIECES_PER_FF, PIECES_PER_DOWN)
    x1buf[...] = x1
    hm = _rms(x1, mix_g[...]).astype(_bf16)
    for lo, which in zip(range(0, D_IN, IN_COLS), PIECES_PER_IN, strict=True):
        ubuf[:, lo:lo + IN_COLS] = _anchored(_dot(hm, w_in[:, lo:lo + IN_COLS]), chains, which)
    assert all(next(chain, None) is None for chain in chains.values())

    @pl.when(s == last)
    def _():
        copies = [pltpu.make_async_copy(x1buf, x1s_hbm, sems.at[0]),
                  pltpu.make_async_copy(ubuf, us_hbm, sems.at[1])]
        for copy in copies:
            copy.start()
        for copy in copies:
            copy.wait()


def _prompt_a_call(x, xs, bsz, pre_g, post_g, wgu, wd, mix_g, w_in, conv_w, conv_b, conv_g, lconv_w, lconv_b,
                   w_gates, ba, bx, lam, w_out, mix_post_g):
    n = x.shape[0]
    tt = TIME_TILE
    tiles = n // tt
    tiles_per_seq = tiles // bsz
    assert xs.shape == (tt, D_MODEL), "the sample group must fill exactly one tile of the first chain"
    consts = (pre_g, post_g, wgu, wd, mix_g, w_in, conv_w, conv_b, conv_g, lconv_w, lconv_b, w_gates, ba, bx,
              lam, w_out, mix_post_g)
    prev_seq = lambda s: (jnp.maximum(s - 1, 0) // tiles_per_seq, 0, 0)
    return pl.pallas_call(
        functools.partial(_prompt_a_kernel, tiles_per_seq=tiles_per_seq),
        grid=(tiles + 1,),
        in_specs=[pl.BlockSpec((tt, D_MODEL), lambda s: (jnp.minimum(s, tiles - 1), 0)), _const_spec(xs.shape)]
                 + [_const_spec(c.shape) for c in consts],
        out_specs=[pl.BlockSpec((tt, D_MODEL), lambda s: (jnp.maximum(s - 1, 0), 0)),
                   pl.BlockSpec((1, CONV_WIDTH - 1, D_CONV), prev_seq),
                   pl.BlockSpec((1, LRU_CONV_WIDTH - 1, D_LRU), prev_seq),
                   pl.BlockSpec((1, 1, D_LRU), prev_seq),
                   pl.BlockSpec(memory_space=pl.ANY),
                   pl.BlockSpec(memory_space=pl.ANY)],
        out_shape=[jax.ShapeDtypeStruct((n, D_MODEL), _f32),
                   jax.ShapeDtypeStruct((bsz, CONV_WIDTH - 1, D_CONV), _f32),
                   jax.ShapeDtypeStruct((bsz, LRU_CONV_WIDTH - 1, D_LRU), _f32),
                   jax.ShapeDtypeStruct((bsz, 1, D_LRU), _f32),
                   jax.ShapeDtypeStruct((tt, D_MODEL), _f32),
                   jax.ShapeDtypeStruct((tt, D_IN), _f32)],
        scratch_shapes=[pltpu.VMEM((tt, D_FF), _bf16),
                        pltpu.VMEM((tt, D_IN), _f32),
                        pltpu.VMEM((tt, D_MODEL), _f32),
                        pltpu.VMEM((CONV_PAD + tt, D_CONV), _f32),
                        pltpu.VMEM((LCONV_PAD + tt, D_LRU), _f32),
                        pltpu.VMEM((tt, D_CONV), _bf16),
                        pltpu.VMEM((tt, D_LRU), _f32),
                        pltpu.VMEM((tt, D_LRU), _f32),
                        pltpu.VMEM((tt, D_LRU), _bf16),
                        pltpu.VMEM((1, D_LRU), _f32),
                        pltpu.SemaphoreType.DMA((2,))],
        compiler_params=pltpu.CompilerParams(dimension_semantics=("arbitrary",),
                                             vmem_limit_bytes=VMEM_LIMIT_BYTES),
        name="prompt_ffn1_mix",
    )(x, xs, *consts)


def _mix_sample_kernel(u_ref, x1_ref, cache_ref, lstate_ref, h0_ref,
                       conv_w, conv_b, conv_g, lconv_w, lconv_b, w_gates, ba, bx, lam, w_out, post_g,
                       x2_ref, conv_st_ref, lconv_st_ref, h_st_ref):
    steps = u_ref.shape[0]
    past = CONV_WIDTH - 1
    lpast = LRU_CONV_WIDTH - 1

    gp = [cache_ref[j] for j in range(past)]
    gp += [u_ref[t, :, 0:D_CONV] * jax.nn.sigmoid(u_ref[t, :, D_CONV:2 * D_CONV]) for t in range(steps)]
    for j in range(past):
        conv_st_ref[j] = gp[j + steps]
    c_rows = []
    for t in range(steps):
        acc = jnp.broadcast_to(conv_b[...], gp[0].shape)
        for k in range(CONV_WIDTH):
            acc = acc + conv_w[k:k + 1, :] * gp[t + k]
        cn = _rms(acc, conv_g[...])
        c_rows.append(cn * jax.nn.sigmoid(cn))
    c = jnp.concatenate(c_rows, axis=0).astype(_bf16)

    xp = [lstate_ref[j] for j in range(lpast)]
    xp += [u_ref[t, :, 2 * D_CONV:2 * D_CONV + D_LRU] for t in range(steps)]
    for j in range(lpast):
        lconv_st_ref[j] = xp[j + steps]
    xc_rows = []
    for t in range(steps):
        acc = jnp.broadcast_to(lconv_b[...], xp[0].shape)
        for k in range(LRU_CONV_WIDTH):
            acc = acc + lconv_w[k:k + 1, :] * xp[t + k]
        xc_rows.append(acc)
    xc = jnp.concatenate(xc_rows, axis=0)
    r, i_g = _lru_gates(_lru_gate_logits(xc, w_gates), ba[...], bx[...])
    a, b = _lru_ab(xc, r, i_g, lam[...])
    bt = h0_ref.shape[0]
    h = h0_ref[...]
    hs = []
    for t in range(steps):
        h = a[t * bt:(t + 1) * bt, :] * h + b[t * bt:(t + 1) * bt, :]
        hs.append(h)
    h_st_ref[...] = h
    u_gelu = jnp.concatenate([u_ref[t, :, 2 * D_CONV + D_LRU:D_IN] for t in range(steps)], axis=0)
    yb = (jnp.concatenate(hs, axis=0) * jax.nn.gelu(u_gelu)).astype(_bf16)

    out = _dot(c, w_out[0:D_CONV, :]) + _dot(yb, w_out[D_CONV:D_CONV + D_LRU, :])
    x2 = _rms(out, post_g[...])
    for t in range(steps):
        x2_ref[t] = x1_ref[t] + x2[t * bt:(t + 1) * bt, :]


def _mix_sample_call(u, x1, cache, lstate, h0, conv_w, conv_b, conv_g, lconv_w, lconv_b, w_gates, ba, bx,
                     lam, w_out, post_g):
    steps, bsz, _ = u.shape
    bt = SAMPLE_BATCH_TILE
    consts = (conv_w, conv_b, conv_g, lconv_w, lconv_b, w_gates, ba, bx, lam, w_out, post_g)
    return pl.pallas_call(
        _mix_sample_kernel,
        grid=(bsz // bt,),
        in_specs=[pl.BlockSpec((steps, bt, D_IN), lambda i: (0, i, 0)),
                  pl.BlockSpec((steps, bt, D_MODEL), lambda i: (0, i, 0)),
                  pl.BlockSpec((CONV_WIDTH - 1, bt, D_CONV), lambda i: (0, i, 0)),
                  pl.BlockSpec((LRU_CONV_WIDTH - 1, bt, D_LRU), lambda i: (0, i, 0)),
                  pl.BlockSpec((bt, D_LRU), lambda i: (i, 0))]
                 + [_const_spec(c.shape) for c in consts],
        out_specs=[pl.BlockSpec((steps, bt, D_MODEL), lambda i: (0, i, 0)),
                   pl.BlockSpec((CONV_WIDTH - 1, bt, D_CONV), lambda i: (0, i, 0)),
                   pl.BlockSpec((LRU_CONV_WIDTH - 1, bt, D_LRU), lambda i: (0, i, 0)),
                   pl.BlockSpec((bt, D_LRU), lambda i: (i, 0))],
        out_shape=[jax.ShapeDtypeStruct((steps, bsz, D_MODEL), _f32),
                   jax.ShapeDtypeStruct((CONV_WIDTH - 1, bsz, D_CONV), _f32),
                   jax.ShapeDtypeStruct((LRU_CONV_WIDTH - 1, bsz, D_LRU), _f32),
                   jax.ShapeDtypeStruct((bsz, D_LRU), _f32)],
        compiler_params=pltpu.CompilerParams(dimension_semantics=("arbitrary",),
                                             vmem_limit_bytes=VMEM_LIMIT_BYTES),
        name="mix_sample",
    )(u, x1, cache, lstate, h0, *consts)


def _block_diag(w):
    heads, d, _ = w.shape
    tiled = jnp.tile(w.reshape(heads * d, d), (1, heads))
    rows = lax.broadcasted_iota(jnp.int32, tiled.shape, 0) // d
    cols = lax.broadcasted_iota(jnp.int32, tiled.shape, 1) // d
    return jnp.where(rows == cols, tiled, 0.0)


def kernel(x_prompt, x_sample, cache_conv, state_lru_conv, state_lru_h, p_prompt, p_sample, ffn1_pre_g, ffn1_post_g, ffn1_w_gu, ffn1_w_down, mix_pre_g, mix_post_g, w_in, conv_w, conv_b, conv_norm_g, lru_conv_w, lru_conv_b, lru_wa, lru_ba, lru_wx, lru_bx, lru_lambda, w_out, ffn2_pre_g, ffn2_post_g, ffn2_w_gu, ffn2_w_down, ple_norm_g, ple_w_gate, ple_w_proj, ple_post_g):
    depth = ffn1_w_gu.shape[0]
    bsz, seq, _ = x_prompt.shape
    dbsz, dseq, _ = x_sample.shape

    xp = x_prompt.reshape(bsz * seq, D_MODEL)
    xs = x_sample.transpose(1, 0, 2).reshape(dseq * dbsz, D_MODEL)
    outs = [[] for _ in range(6)]
    for l in range(depth):
        row = lambda v: v[l:l + 1]
        wgu1, wd1 = ffn1_w_gu[l].astype(_bf16), ffn1_w_down[l].astype(_bf16)
        wgu2, wd2 = ffn2_w_gu[l].astype(_bf16), ffn2_w_down[l].astype(_bf16)
        w_in_l, w_out_l = w_in[l].astype(_bf16), w_out[l].astype(_bf16)
        w_gate_l, w_proj_l = ple_w_gate[l].astype(_bf16), ple_w_proj[l].astype(_bf16)
        w_gates = jnp.concatenate([_block_diag(lru_wa[l]), _block_diag(lru_wx[l])], axis=1).astype(_bf16)
        conv_consts = (conv_w[l], row(conv_b), row(conv_norm_g))
        lru_consts = (lru_conv_w[l], row(lru_conv_b), w_gates, row(lru_ba), row(lru_bx), row(lru_lambda))
        ffn1_args = (row(ffn1_pre_g), row(ffn1_post_g), wgu1, wd1, row(mix_pre_g), w_in_l)
        ffn2_args = (row(ffn2_pre_g), row(ffn2_post_g), wgu2, wd2, row(ple_norm_g), w_gate_l, w_proj_l,
                     row(ple_post_g))

        x2, cst, lst, hst, x1, u = _prompt_a_call(xp, xs, bsz, *ffn1_args, *conv_consts, *lru_consts, w_out_l,
                                                  row(mix_post_g))
        outs[0].append(cst); outs[1].append(lst); outs[2].append(hst.reshape(bsz, D_LRU))

        x2s, cst, lst, hst = _mix_sample_call(u.reshape(dseq, dbsz, D_IN), x1.reshape(dseq, dbsz, D_MODEL),
                                              cache_conv[l].transpose(1, 0, 2),
                                              state_lru_conv[l].transpose(1, 0, 2), state_lru_h[l],
                                              *conv_consts, *lru_consts, w_out_l, row(mix_post_g))
        outs[3].append(cst.transpose(1, 0, 2)); outs[4].append(lst.transpose(1, 0, 2)); outs[5].append(hst)

        ps = p_sample[l].transpose(1, 0, 2).reshape(dseq * dbsz, D_PLE)
        xp, xs = _ffn_ple_call(x2, p_prompt[l].reshape(bsz * seq, D_PLE), x2s.reshape(dseq * dbsz, D_MODEL), ps,
                               *ffn2_args)

    y_prompt = xp.reshape(bsz, seq, D_MODEL)
    y_sample = xs.reshape(dseq, dbsz, D_MODEL).transpose(1, 0, 2)
    return (y_prompt, y_sample) + tuple(o[0][None] if depth == 1 else jnp.stack(o) for o in outs)
```

```python
import functools

import jax
import jax.numpy as jnp
from jax import lax
from jax.experimental import pallas as pl
from jax.experimental.pallas import tpu as pltpu

D_MODEL = 1024
D_CONV = 512
D_LRU = 512
D_IN = 2 * D_CONV + 2 * D_LRU
D_FF = 2816
D_PLE = 256
LRU_HEADS = 8
CONV_WIDTH = 31
LRU_CONV_WIDTH = 4
LRU_C = 8.0
EPS = 1e-6
FFN_RES_WEIGHT = 0.5

SUBLANES = 8
LANES = 128
MXU_TILE = 256
VMEM_LIMIT_BYTES = 56 * 1024 * 1024
FFN_PLE_VMEM_LIMIT_BYTES = 60 * 1024 * 1024

TOKEN_TILE = 1024
TIME_TILE = 512
CONV_ROWS = 64
CONV_PAD = 32
LCONV_PAD = 8
SAMPLE_BATCH_TILE = 64
ANCHOR_ROWS = 16

FF_CHUNKS = tuple((lo, min(512, D_FF - lo)) for lo in range(0, D_FF, 512))

FF_COLS = 256
DOWN_COLS = 256
IN_COLS = 512
PART_ROWS = 128
PIECES_PER_FF = ("CCCC", "LC", "LC", "LC", "LC", "LC", "LC", "LC", "LC", "LC", "LC")
PIECES_PER_DOWN = ("", "LLCCC", "LLCCC", "LLCC")
PIECES_PER_IN = ("LLLLCCCC", "LCCCCCC", "LLLLCCCC", "")

_bf16 = jnp.bfloat16
_f32 = jnp.float32


def _rms(x, g):
    ms = jnp.mean(x * x, axis=-1, keepdims=True)
    return x * lax.rsqrt(ms + EPS) * g


def _dot(a, b):
    return jnp.dot(a, b, preferred_element_type=_f32)


def _ffn_ple_stages(x_ref, p_ref, pre_g, post_g, wgu_ref, wd_ref, ple_g, w_gate, w_proj, ple_post_g, y_ref,
                    act_ref):
    h = _rms(x_ref[...], pre_g[...]).astype(_bf16)
    yield
    for lo, n in FF_CHUNKS:
        gate = _dot(h, wgu_ref[:, lo:lo + n])
        up = _dot(h, wgu_ref[:, D_FF + lo:D_FF + lo + n])
        act_ref[:, lo:lo + n] = (gate * jax.nn.sigmoid(gate) * up).astype(_bf16)
    yield
    x3 = x_ref[...] + FFN_RES_WEIGHT * _rms(_dot(act_ref[...], wd_ref[...]), post_g[...])
    yield
    gate = jax.nn.sigmoid(_dot(_rms(x3, ple_g[...]).astype(_bf16), w_gate[...]))
    e = _dot(p_ref[...].astype(_bf16), w_proj[...])
    yield
    y_ref[...] = x3 + _rms(gate * e, ple_post_g[...])
    yield


HALF_STAGE_ORDER = "AABABABABB"


def _ffn_ple_block(x_ref, p_ref, y_ref, act_ref, weights):
    half = x_ref.shape[0] // 2
    stages = {name: _ffn_ple_stages(x_ref.at[lo:lo + half], p_ref.at[lo:lo + half], *weights,
                                    y_ref.at[lo:lo + half], act_ref.at[lo:lo + half])
              for name, lo in (("A", 0), ("B", half))}
    for name in HALF_STAGE_ORDER:
        next(stages[name])


def _ffn_ple_kernel(x_ref, p_ref, xs_ref, ps_ref, pre_g, post_g, wgu, wd, ple_g, w_gate, w_proj, ple_post_g,
                    y_ref, ys_ref, act_ref):
    weights = (pre_g, post_g, wgu, wd, ple_g, w_gate, w_proj, ple_post_g)
    last = pl.num_programs(0) - 1

    @pl.when(pl.program_id(0) < last)
    def _():
        _ffn_ple_block(x_ref, p_ref, y_ref, act_ref, weights)

    @pl.when(pl.program_id(0) == last)
    def _():
        _ffn_ple_block(xs_ref, ps_ref, ys_ref, act_ref.at[0:xs_ref.shape[0]], weights)


def _const_spec(shape):
    zeros = (0,) * len(shape)
    return pl.BlockSpec(shape, lambda *_: zeros, pipeline_mode=pl.Buffered(1))


def _ffn_ple_call(x, p, xs, ps, pre_g, post_g, wgu, wd, ple_g, w_gate, w_proj, ple_post_g):
    n, ns = x.shape[0], xs.shape[0]
    tile = TOKEN_TILE
    tiles = n // tile
    assert ns <= tile
    consts = (pre_g, post_g, wgu, wd, ple_g, w_gate, w_proj, ple_post_g)
    rows = lambda width: pl.BlockSpec((tile, width), lambda i: (jnp.minimum(i, tiles - 1), 0))
    return pl.pallas_call(
        _ffn_ple_kernel,
        grid=(tiles + 1,),
        in_specs=[rows(D_MODEL), rows(D_PLE), _const_spec(xs.shape), _const_spec(ps.shape)]
                 + [_const_spec(c.shape) for c in consts],
        out_specs=[rows(D_MODEL), pl.BlockSpec((ns, D_MODEL), lambda i: (0, 0))],
        out_shape=[jax.ShapeDtypeStruct((n, D_MODEL), _f32), jax.ShapeDtypeStruct((ns, D_MODEL), _f32)],
        scratch_shapes=[pltpu.VMEM((tile, D_FF), _bf16)],
        compiler_params=pltpu.CompilerParams(dimension_semantics=("arbitrary",),
                                             vmem_limit_bytes=FFN_PLE_VMEM_LIMIT_BYTES),
        name="ffn2_ple",
    )(x, p, xs, ps, *consts)


def _lru_gate_logits(xc, w_gates_ref):
    assert MXU_TILE % (D_LRU // LRU_HEADS) == 0
    xb = xc.astype(_bf16)
    cols = []
    for gate in range(2):
        for lo in range(0, D_LRU, MXU_TILE):
            cols.append(_dot(xb[:, lo:lo + MXU_TILE],
                             w_gates_ref[lo:lo + MXU_TILE, gate * D_LRU + lo:gate * D_LRU + lo + MXU_TILE]))
    return jnp.concatenate(cols, axis=1)


def _lru_gates(logits, ba, bx):
    return jax.nn.sigmoid(logits[:, :D_LRU] + ba), jax.nn.sigmoid(logits[:, D_LRU:] + bx)


def _lru_ab(xc, r, i_g, lam):
    neg_lam = -lam
    softplus = jnp.maximum(neg_lam, 0.0) + jnp.log1p(jnp.exp(-jnp.abs(neg_lam)))
    log_a = -LRU_C * r * softplus
    a = jnp.exp(log_a)
    b = jnp.sqrt(-jnp.tanh(log_a) * (a * a + 1.0)) * (i_g * xc)
    return a, b


def _causal_taps(buf, r0, c0, rows, pad, width, w_ref, b_ref):
    first = pad - (width - 1)
    x = buf[r0:r0 + rows + pad, c0:c0 + LANES]
    acc = jnp.broadcast_to(b_ref[:, c0:c0 + LANES], (rows, LANES))
    for phase in range(SUBLANES):
        taps = [(q, q * SUBLANES + phase - first) for q in range(pad // SUBLANES + 1)]
        taps = [(q, k) for q, k in taps if 0 <= k < width]
        if not taps:
            continue
        xs = x if phase == 0 else pltpu.roll(x, rows + pad - phase, axis=0)
        for q, k in taps:
            acc = acc + w_ref[k:k + 1, c0:c0 + LANES] * xs[q * SUBLANES:q * SUBLANES + rows, :]
    return acc


def _anchored(value, chains, which):
    pieces = [p for c in which for p in next(chains[c])]
    if callable(value):
        value = value()
    if not pieces:
        return value
    folded = []
    for p in pieces:
        m = jnp.min(p.reshape(-1, SUBLANES, p.shape[-1]), axis=0)
        folded += [m[:, c0:c0 + LANES] for c0 in range(0, m.shape[1], LANES)]
    zero = jnp.minimum(jnp.abs(functools.reduce(jnp.minimum, folded)), 0.0)
    zero = jnp.concatenate([zero] * (ANCHOR_ROWS // SUBLANES), axis=0).astype(value.dtype)
    zero = jnp.concatenate([zero] * (value.shape[1] // LANES), axis=1)
    split = value.shape[0] - ANCHOR_ROWS
    return jnp.concatenate([value[0:split], value[split:] + zero], axis=0)


def _ffn_anchored(x, pre_g, post_g, wgu_ref, wd_ref, act_ref, chains, per_ff, per_down):
    h = _rms(x, pre_g).astype(_bf16)
    for lo, which in zip(range(0, D_FF, FF_COLS), per_ff, strict=True):
        gate = _dot(h, wgu_ref[:, lo:lo + FF_COLS])
        up = _dot(h, wgu_ref[:, D_FF + lo:D_FF + lo + FF_COLS])
        act = _anchored(lambda: gate * jax.nn.sigmoid(gate) * up, chains, which)
        act_ref[:, lo:lo + FF_COLS] = act.astype(_bf16)
    ys = [_dot(_anchored(act_ref[...], chains, which), wd_ref[:, lo:lo + DOWN_COLS])
          for lo, which in zip(range(0, D_MODEL, DOWN_COLS), per_down, strict=True)]
    return x + FFN_RES_WEIGHT * _rms(jnp.concatenate(ys, axis=1), post_g)


def _lru_steps(fresh, lconv_w, lconv_b, w_gates, ba, bx, lam, ubuf, lconv_st_ref, h_st_ref,
               xbuf, xcbuf, hbuf, ybuf, hcar):
    tt = ubuf.shape[0]
    parts = [(lo, lo + PART_ROWS) for lo in range(0, tt, PART_ROWS)]

    for lo, hi in parts:
        blocks = []
        for r0 in range(lo, hi, CONV_ROWS):
            for c0 in range(0, D_LRU, LANES):
                xc = _causal_taps(xbuf, r0, c0, CONV_ROWS, LCONV_PAD, LRU_CONV_WIDTH, lconv_w, lconv_b)
                xcbuf[r0:r0 + CONV_ROWS, c0:c0 + LANES] = xc
                blocks.append(xc)
        yield blocks
    lconv_st_ref[0] = xbuf[pl.ds(LCONV_PAD + tt - (LRU_CONV_WIDTH - 1), LRU_CONV_WIDTH - 1), :]

    gates = []
    for lo, hi in parts:
        xc = xcbuf[lo:hi, :]
        r, i_g = _lru_gates(_lru_gate_logits(xc, w_gates), ba[...], bx[...])
        gates.append((xc, r, i_g))
        yield [r, i_g]
    coeffs = []
    for xc, r, i_g in gates:
        a, b = _lru_ab(xc, r, i_g, lam[...])
        coeffs.append((a, b))
        yield [b]

    scanned = []
    for a, b in coeffs:
        for lo in range(0, PART_ROWS, CONV_ROWS):
            aq = a[lo:lo + CONV_ROWS].reshape(CONV_ROWS // SUBLANES, SUBLANES, D_LRU)
            bq = b[lo:lo + CONV_ROWS].reshape(aq.shape)
            row = lax.broadcasted_iota(jnp.int32, aq.shape, 1)
            shift = 1
            while shift < SUBLANES:
                keep = row >= shift
                bq = jnp.where(keep, aq * pltpu.roll(bq, shift, axis=1) + bq, bq)
                aq = jnp.where(keep, aq * pltpu.roll(aq, shift, axis=1), aq)
                shift *= 2
            scanned.append((aq, bq))
            yield [aq, bq]

    carry = jnp.where(fresh, 0.0, hcar[...])
    r0 = 0
    for aq, bq in scanned:
        for i in range(aq.shape[0]):
            rows = aq[i] * carry + bq[i]
            hbuf[r0:r0 + SUBLANES, :] = rows
            carry = rows[SUBLANES - 1:SUBLANES, :]
            r0 += SUBLANES
    hcar[...] = carry
    h_st_ref[0] = carry
    yield [rows]

    for lo, hi in parts:
        yb = hbuf[lo:hi, :] * jax.nn.gelu(ubuf[lo:hi, 2 * D_CONV + D_LRU:D_IN])
        ybuf[lo:hi, :] = yb.astype(_bf16)
        yield [yb]


def _conv_steps(conv_w, conv_b, conv_g, ubuf, conv_st_ref, gbuf, cbuf):
    tt = cbuf.shape[0]
    for lo in range(0, tt, PART_ROWS):
        g = ubuf[lo:lo + PART_ROWS, 0:D_CONV] * jax.nn.sigmoid(ubuf[lo:lo + PART_ROWS, D_CONV:2 * D_CONV])
        gbuf[CONV_PAD + lo:CONV_PAD + lo + PART_ROWS, :] = g
        yield [g]
    conv_st_ref[0] = gbuf[pl.ds(CONV_PAD + tt - (CONV_WIDTH - 1), CONV_WIDTH - 1), :]
    for r0 in range(0, tt, CONV_ROWS):
        cols = []
        for c0 in range(0, D_CONV, LANES):
            cols.append(_causal_taps(gbuf, r0, c0, CONV_ROWS, CONV_PAD, CONV_WIDTH, conv_w, conv_b))
            if len(cols) * LANES < D_CONV:
                yield [cols[-1]]
        cn = _rms(jnp.concatenate(cols, axis=1), conv_g[...])
        c = cn * jax.nn.sigmoid(cn)
        cbuf[r0:r0 + CONV_ROWS, :] = c.astype(_bf16)
        yield [c]


def _out_steps(w_out, post_g, x2_ref, cbuf, ybuf):
    out = _dot(cbuf[...], w_out[0:D_CONV, :]) + _dot(ybuf[...], w_out[D_CONV:D_CONV + D_LRU, :])
    delta = _rms(out, post_g[...])
    x2_ref[...] += delta
    yield [delta]


def _prompt_a_kernel(x_ref, xs_ref, pre_g, post_g, wgu, wd, mix_g, w_in,
                     conv_w, conv_b, conv_g, lconv_w, lconv_b, w_gates, ba, bx, lam, w_out, mix_post_g,
                     x2_ref, conv_st_ref, lconv_st_ref, h_st_ref, x1s_hbm, us_hbm,
                     act_ref, ubuf, x1buf, gbuf, xbuf, cbuf, xcbuf, hbuf, ybuf, hcar, sems, *, tiles_per_seq):
    s = pl.program_id(0)
    last = pl.num_programs(0) - 1
    tt = x_ref.shape[0]

    @pl.when(s == 0)
    def _():
        ubuf[...] = jnp.zeros(ubuf.shape, _f32)
        x1buf[...] = jnp.zeros(x1buf.shape, _f32)
        gbuf[...] = jnp.zeros(gbuf.shape, _f32)
        xbuf[...] = jnp.zeros(xbuf.shape, _f32)
        hcar[...] = jnp.zeros(hcar.shape, _f32)

    fresh = lax.rem(s + tiles_per_seq - 1, tiles_per_seq) == 0
    x2_ref[...] = x1buf[...]
    gbuf[0:CONV_PAD, :] = jnp.where(fresh, 0.0, gbuf[tt:tt + CONV_PAD, :])
    xbuf[0:LCONV_PAD, :] = jnp.where(fresh, 0.0, xbuf[tt:tt + LCONV_PAD, :])
    xbuf[LCONV_PAD:LCONV_PAD + tt, :] = ubuf[:, 2 * D_CONV:2 * D_CONV + D_LRU]
    chains = {
        "L": _lru_steps(fresh, lconv_w, lconv_b, w_gates, ba, bx, lam, ubuf, lconv_st_ref, h_st_ref,
                        xbuf, xcbuf, hbuf, ybuf, hcar),
        "C": _conv_steps(conv_w, conv_b, conv_g, ubuf, conv_st_ref, gbuf, cbuf),
        "F": _out_steps(w_out, mix_post_g, x2_ref, cbuf, ybuf),
    }

    x = jnp.where(s == last, xs_ref[...], x_ref[...])
    x1 = _ffn_anchored(x, pre_g[...], post_g[...], wgu, wd, act_ref, chains, PIECES_PER_FF, PIECES_PER_DOWN)
    x1buf[...] = x1
    hm = _rms(x1, mix_g[...]).astype(_bf16)
    for lo, which in zip(range(0, D_IN, IN_COLS), PIECES_PER_IN, strict=True):
        ubuf[:, lo:lo + IN_COLS] = _anchored(_dot(hm, w_in[:, lo:lo + IN_COLS]), chains, which)
    next(chains["F"])
    assert all(next(chain, None) is None for chain in chains.values())

    @pl.when(s == last)
    def _():
        copies = [pltpu.make_async_copy(x1buf, x1s_hbm, sems.at[0]),
                  pltpu.make_async_copy(ubuf, us_hbm, sems.at[1])]
        for priority, copy in enumerate(copies):
            copy.start(priority=priority)
        for copy in copies:
            copy.wait()


def _prompt_a_call(x, xs, bsz, pre_g, post_g, wgu, wd, mix_g, w_in, conv_w, conv_b, conv_g, lconv_w, lconv_b,
                   w_gates, ba, bx, lam, w_out, mix_post_g):
    n = x.shape[0]
    tt = TIME_TILE
    tiles = n // tt
    tiles_per_seq = tiles // bsz
    assert xs.shape == (tt, D_MODEL), "the sample group must fill exactly one tile of the first chain"
    consts = (pre_g, post_g, wgu, wd, mix_g, w_in, conv_w, conv_b, conv_g, lconv_w, lconv_b, w_gates, ba, bx,
              lam, w_out, mix_post_g)
    prev_seq = lambda s: (jnp.maximum(s - 1, 0) // tiles_per_seq, 0, 0)
    return pl.pallas_call(
        functools.partial(_prompt_a_kernel, tiles_per_seq=tiles_per_seq),
        grid=(tiles + 1,),
        in_specs=[pl.BlockSpec((tt, D_MODEL), lambda s: (jnp.minimum(s, tiles - 1), 0)), _const_spec(xs.shape)]
                 + [_const_spec(c.shape) for c in consts],
        out_specs=[pl.BlockSpec((tt, D_MODEL), lambda s: (jnp.maximum(s - 1, 0), 0)),
                   pl.BlockSpec((1, CONV_WIDTH - 1, D_CONV), prev_seq),
                   pl.BlockSpec((1, LRU_CONV_WIDTH - 1, D_LRU), prev_seq),
                   pl.BlockSpec((1, 1, D_LRU), prev_seq),
                   pl.BlockSpec(memory_space=pl.ANY),
                   pl.BlockSpec(memory_space=pl.ANY)],
        out_shape=[jax.ShapeDtypeStruct((n, D_MODEL), _f32),
                   jax.ShapeDtypeStruct((bsz, CONV_WIDTH - 1, D_CONV), _f32),
                   jax.ShapeDtypeStruct((bsz, LRU_CONV_WIDTH - 1, D_LRU), _f32),
                   jax.ShapeDtypeStruct((bsz, 1, D_LRU), _f32),
                   jax.ShapeDtypeStruct((tt, D_MODEL), _f32),
                   jax.ShapeDtypeStruct((tt, D_IN), _f32)],
        scratch_shapes=[pltpu.VMEM((tt, D_FF), _bf16),
                        pltpu.VMEM((tt, D_IN), _f32),
                        pltpu.VMEM((tt, D_MODEL), _f32),
                        pltpu.VMEM((CONV_PAD + tt, D_CONV), _f32),
                        pltpu.VMEM((LCONV_PAD + tt, D_LRU), _f32),
                        pltpu.VMEM((tt, D_CONV), _bf16),
                        pltpu.VMEM((tt, D_LRU), _f32),
                        pltpu.VMEM((tt, D_LRU), _f32),
                        pltpu.VMEM((tt, D_LRU), _bf16),
                        pltpu.VMEM((1, D_LRU), _f32),
                        pltpu.SemaphoreType.DMA((2,))],
        compiler_params=pltpu.CompilerParams(dimension_semantics=("arbitrary",),
                                             vmem_limit_bytes=VMEM_LIMIT_BYTES),
        name="prompt_ffn1_mix",
    )(x, xs, *consts)


def _mix_sample_kernel(u_ref, x1_ref, cache_ref, lstate_ref, h0_ref,
                       conv_w, conv_b, conv_g, lconv_w, lconv_b, w_gates, ba, bx, lam, w_out, post_g,
                       x2_ref, conv_st_ref, lconv_st_ref, h_st_ref):
    steps = u_ref.shape[0]
    past = CONV_WIDTH - 1
    lpast = LRU_CONV_WIDTH - 1

    gp = [cache_ref[j] for j in range(past)]
    gp += [u_ref[t, :, 0:D_CONV] * jax.nn.sigmoid(u_ref[t, :, D_CONV:2 * D_CONV]) for t in range(steps)]
    for j in range(past):
        conv_st_ref[j] = gp[j + steps]
    c_rows = []
    for t in range(steps):
        acc = jnp.broadcast_to(conv_b[...], gp[0].shape)
        for k in range(CONV_WIDTH):
            acc = acc + conv_w[k:k + 1, :] * gp[t + k]
        cn = _rms(acc, conv_g[...])
        c_rows.append(cn * jax.nn.sigmoid(cn))
    c = jnp.concatenate(c_rows, axis=0).astype(_bf16)

    xp = [lstate_ref[j] for j in range(lpast)]
    xp += [u_ref[t, :, 2 * D_CONV:2 * D_CONV + D_LRU] for t in range(steps)]
    for j in range(lpast):
        lconv_st_ref[j] = xp[j + steps]
    xc_rows = []
    for t in range(steps):
        acc = jnp.broadcast_to(lconv_b[...], xp[0].shape)
        for k in range(LRU_CONV_WIDTH):
            acc = acc + lconv_w[k:k + 1, :] * xp[t + k]
        xc_rows.append(acc)
    xc = jnp.concatenate(xc_rows, axis=0)
    r, i_g = _lru_gates(_lru_gate_logits(xc, w_gates), ba[...], bx[...])
    a, b = _lru_ab(xc, r, i_g, lam[...])
    bt = h0_ref.shape[0]
    h = h0_ref[...]
    hs = []
    for t in range(steps):
        h = a[t * bt:(t + 1) * bt, :] * h + b[t * bt:(t + 1) * bt, :]
        hs.append(h)
    h_st_ref[...] = h
    u_gelu = jnp.concatenate([u_ref[t, :, 2 * D_CONV + D_LRU:D_IN] for t in range(steps)], axis=0)
    yb = (jnp.concatenate(hs, axis=0) * jax.nn.gelu(u_gelu)).astype(_bf16)

    out = _dot(c, w_out[0:D_CONV, :]) + _dot(yb, w_out[D_CONV:D_CONV + D_LRU, :])
    x2 = _rms(out, post_g[...])
    for t in range(steps):
        x2_ref[t] = x1_ref[t] + x2[t * bt:(t + 1) * bt, :]


def _mix_sample_call(u, x1, cache, lstate, h0, conv_w, conv_b, conv_g, lconv_w, lconv_b, w_gates, ba, bx,
                     lam, w_out, post_g):
    steps, bsz, _ = u.shape
    bt = SAMPLE_BATCH_TILE
    consts = (conv_w, conv_b, conv_g, lconv_w, lconv_b, w_gates, ba, bx, lam, w_out, post_g)
    return pl.pallas_call(
        _mix_sample_kernel,
        grid=(bsz // bt,),
        in_specs=[pl.BlockSpec((steps, bt, D_IN), lambda i: (0, i, 0)),
                  pl.BlockSpec((steps, bt, D_MODEL), lambda i: (0, i, 0)),
                  pl.BlockSpec((CONV_WIDTH - 1, bt, D_CONV), lambda i: (0, i, 0)),
                  pl.BlockSpec((LRU_CONV_WIDTH - 1, bt, D_LRU), lambda i: (0, i, 0)),
                  pl.BlockSpec((bt, D_LRU), lambda i: (i, 0))]
                 + [_const_spec(c.shape) for c in consts],
        out_specs=[pl.BlockSpec((steps, bt, D_MODEL), lambda i: (0, i, 0)),
                   pl.BlockSpec((CONV_WIDTH - 1, bt, D_CONV), lambda i: (0, i, 0)),
                   pl.BlockSpec((LRU_CONV_WIDTH - 1, bt, D_LRU), lambda i: (0, i, 0)),
                   pl.BlockSpec((bt, D_LRU), lambda i: (i, 0))],
        out_shape=[jax.ShapeDtypeStruct((steps, bsz, D_MODEL), _f32),
                   jax.ShapeDtypeStruct((CONV_WIDTH - 1, bsz, D_CONV), _f32),
                   jax.ShapeDtypeStruct((LRU_CONV_WIDTH - 1, bsz, D_LRU), _f32),
                   jax.ShapeDtypeStruct((bsz, D_LRU), _f32)],
        compiler_params=pltpu.CompilerParams(dimension_semantics=("arbitrary",),
                                             vmem_limit_bytes=VMEM_LIMIT_BYTES),
        name="mix_sample",
    )(u, x1, cache, lstate, h0, *consts)


def _block_diag(w):
    heads, d, _ = w.shape
    tiled = jnp.tile(w.reshape(heads * d, d), (1, heads))
    rows = lax.broadcasted_iota(jnp.int32, tiled.shape, 0) // d
    cols = lax.broadcasted_iota(jnp.int32, tiled.shape, 1) // d
    return jnp.where(rows == cols, tiled, 0.0)


def kernel(x_prompt, x_sample, cache_conv, state_lru_conv, state_lru_h, p_prompt, p_sample, ffn1_pre_g, ffn1_post_g, ffn1_w_gu, ffn1_w_down, mix_pre_g, mix_post_g, w_in, conv_w, conv_b, conv_norm_g, lru_conv_w, lru_conv_b, lru_wa, lru_ba, lru_wx, lru_bx, lru_lambda, w_out, ffn2_pre_g, ffn2_post_g, ffn2_w_gu, ffn2_w_down, ple_norm_g, ple_w_gate, ple_w_proj, ple_post_g):
    depth = ffn1_w_gu.shape[0]
    bsz, seq, _ = x_prompt.shape
    dbsz, dseq, _ = x_sample.shape

    xp = x_prompt.reshape(bsz * seq, D_MODEL)
    xs = x_sample.transpose(1, 0, 2).reshape(dseq * dbsz, D_MODEL)
    outs = [[] for _ in range(6)]
    for l in range(depth):
        row = lambda v: v[l:l + 1]
        wgu1, wd1 = ffn1_w_gu[l].astype(_bf16), ffn1_w_down[l].astype(_bf16)
        wgu2, wd2 = ffn2_w_gu[l].astype(_bf16), ffn2_w_down[l].astype(_bf16)
        w_in_l, w_out_l = w_in[l].astype(_bf16), w_out[l].astype(_bf16)
        w_gate_l, w_proj_l = ple_w_gate[l].astype(_bf16), ple_w_proj[l].astype(_bf16)
        w_gates = jnp.concatenate([_block_diag(lru_wa[l]), _block_diag(lru_wx[l])], axis=1).astype(_bf16)
        conv_consts = (conv_w[l], row(conv_b), row(conv_norm_g))
        lru_consts = (lru_conv_w[l], row(lru_conv_b), w_gates, row(lru_ba), row(lru_bx), row(lru_lambda))
        ffn1_args = (row(ffn1_pre_g), row(ffn1_post_g), wgu1, wd1, row(mix_pre_g), w_in_l)
        ffn2_args = (row(ffn2_pre_g), row(ffn2_post_g), wgu2, wd2, row(ple_norm_g), w_gate_l, w_proj_l,
                     row(ple_post_g))

        x2, cst, lst, hst, x1, u = _prompt_a_call(xp, xs, bsz, *ffn1_args, *conv_consts, *lru_consts, w_out_l,
                                                  row(mix_post_g))
        outs[0].append(cst); outs[1].append(lst); outs[2].append(hst.reshape(bsz, D_LRU))

        x2s, cst, lst, hst = _mix_sample_call(u.reshape(dseq, dbsz, D_IN), x1.reshape(dseq, dbsz, D_MODEL),
                                              cache_conv[l].transpose(1, 0, 2),
                                              state_lru_conv[l].transpose(1, 0, 2), state_lru_h[l],
                                              *conv_consts, *lru_consts, w_out_l, row(mix_post_g))
        outs[3].append(cst.transpose(1, 0, 2)); outs[4].append(lst.transpose(1, 0, 2)); outs[5].append(hst)

        ps = p_sample[l].transpose(1, 0, 2).reshape(dseq * dbsz, D_PLE)
        xp, xs = _ffn_ple_call(x2, p_prompt[l].reshape(bsz * seq, D_PLE), x2s.reshape(dseq * dbsz, D_MODEL), ps,
                               *ffn2_args)

    y_prompt = xp.reshape(bsz, seq, D_MODEL)
    y_sample = xs.reshape(dseq, dbsz, D_MODEL).transpose(1, 0, 2)
    return (y_prompt, y_sample) + tuple(o[0][None] if depth == 1 else jnp.stack(o) for o in outs)
```
